```python
import jax, jax.numpy as jnp
from jax import lax
import numpy as np

D_MODEL = 1024
BATCH = 8
SEQ = 2048
DEPTH = 2

N_MIXERS = 4
GROUP_W = D_MODEL // N_MIXERS
HEAD_DIM = 64
N_HEADS = GROUP_W // HEAD_DIM
D_FF = 2816
FFN_RESID = 0.5
SC_WIDTH = 3
CHUNK = 128
CM_WIDTH = 31
RANK_W = 32
RANK_A = 32
RANK_G = 64
A_COLS = 3 * GROUP_W
B_COLS = 2 * GROUP_W
C_COLS = 3 * GROUP_W + RANK_W + RANK_A + RANK_G
D_COLS = 2 * GROUP_W
IN_COLS = A_COLS + B_COLS + C_COLS + D_COLS
RMS_EPS = 1e-6
LN_EPS = 1e-5
GN_EPS = 1e-5 * HEAD_DIM

kernel_name = "hybrid_headgroup_conv_gmlp_rwkv7_conformer"


def rms_norm(x, g):
    xf = x.astype(jnp.float32)
    y = xf * lax.rsqrt(jnp.mean(xf * xf, axis=-1, keepdims=True) + RMS_EPS)
    return (y * g.astype(jnp.float32)).astype(x.dtype)


def layer_norm(x, g, b, eps=LN_EPS):
    xf = x.astype(jnp.float32)
    mu = jnp.mean(xf, axis=-1, keepdims=True)
    var = jnp.mean(jnp.square(xf - mu), axis=-1, keepdims=True)
    y = (xf - mu) * lax.rsqrt(var + eps) * g.astype(jnp.float32) + b.astype(jnp.float32)
    return y.astype(x.dtype)


def swiglu(h, w_gate, w_up, w_down):
    return (jax.nn.silu(h @ w_gate) * (h @ w_up)) @ w_down


def token_shift(t):
    return jnp.pad(t, ((0, 0), (1, 0), (0, 0)))[:, :-1, :]


def causal_depthwise_conv(x, w):
    K, C = w.shape
    return lax.conv_general_dilated(
        x, w[:, None, :].astype(x.dtype), window_strides=(1,), padding=[(K - 1, 0)],
        dimension_numbers=("NWC", "WIO", "NWC"), feature_group_count=C)


def spatial_gating(u, v, w_s, b_s, ln_w, ln_b):
    Bsz, S, G = v.shape
    v = layer_norm(v, ln_w, ln_b)
    v = v.reshape(Bsz, S // CHUNK, CHUNK, N_HEADS, G // N_HEADS)
    w = w_s * jnp.tril(jnp.ones((CHUNK, CHUNK), w_s.dtype))
    s = jnp.einsum("hts,bnshd->bnthd", w, v) + b_s.T[None, None, :, :, None]
    return u * s.reshape(Bsz, S, G)


def rwkv7_recurrence(r, w, k, v, a, b):
    Bsz, S, H, N = r.shape
    seq = tuple(jnp.swapaxes(t.astype(jnp.float32), 0, 1) for t in (r, w, k, v, a, b))

    def step(state, inp):
        r_t, w_t, k_t, v_t, a_t, b_t = inp
        sa = jnp.einsum("bhij,bhj->bhi", state, a_t)
        state = (state * w_t[:, :, None, :] + sa[..., None] * b_t[:, :, None, :]
                 + v_t[..., None] * k_t[:, :, None, :])
        y = jnp.einsum("bhij,bhj->bhi", state, r_t)
        return state, y

    s0 = jnp.zeros((Bsz, H, N, N), jnp.float32)
    _, ys = lax.scan(step, s0, seq)
    return jnp.swapaxes(ys, 0, 1)


def rwkv7_time_mix(pc, mu, w0, w_up, a0, a_up, g_up, k_k, k_a, r_k, ln_w, ln_b):
    Bsz, S, _ = pc.shape
    pc = pc + (token_shift(pc) - pc) * mu
    G = GROUP_W
    r, k, v, wd, ad, gd = jnp.split(
        pc, [G, 2 * G, 3 * G, 3 * G + RANK_W, 3 * G + RANK_W + RANK_A], axis=-1)
    w_log = -jax.nn.softplus(-(w0 + jnp.tanh(wd) @ w_up)) - 0.5
    decay = jnp.exp(-jnp.exp(w_log.astype(jnp.float32)))
    a = jax.nn.sigmoid(a0 + ad @ a_up)
    g = jax.nn.sigmoid(gd) @ g_up
    heads = lambda t: t.reshape(Bsz, S, N_HEADS, HEAD_DIM)
    kk = heads(k * k_k).astype(jnp.float32)
    kk = kk / jnp.maximum(jnp.sqrt(jnp.sum(kk * kk, axis=-1, keepdims=True)), 1e-12)
    k = k * (1.0 + (a - 1.0) * k_a)
    rh, kh, vh, ah = heads(r), heads(k), heads(v), heads(a)
    o = rwkv7_recurrence(rh, heads(decay), kh, vh, -kk, kk * ah).astype(pc.dtype)
    o = layer_norm(o, ln_w.reshape(N_HEADS, HEAD_DIM), ln_b.reshape(N_HEADS, HEAD_DIM), eps=GN_EPS)
    o = o + jnp.sum(rh * kh * r_k, axis=-1, keepdims=True) * vh
    return o.reshape(Bsz, S, G) * g


def hybrid_mix(h, w_in, sc_conv_w, sg_ln_w, sg_ln_b, sg_w, sg_b,
               rk_mu, rk_w0, rk_w_up, rk_a0, rk_a_up, rk_g_up, rk_k_k, rk_k_a, rk_r_k,
               rk_ln_w, rk_ln_b, cm_conv_w, cm_conv_b, cm_ln_w, cm_ln_b, w_out):
    p = h @ w_in
    pa, pb, pc, pd = jnp.split(p, [A_COLS, A_COLS + B_COLS, A_COLS + B_COLS + C_COLS], axis=-1)
    gate_b, gate_c, xa = jnp.split(pa, 3, axis=-1)
    y_a = gate_b * causal_depthwise_conv(gate_c * xa, sc_conv_w)
    u, v = jnp.split(pb, 2, axis=-1)
    y_b = spatial_gating(u, v, sg_w, sg_b, sg_ln_w, sg_ln_b)
    y_c = rwkv7_time_mix(pc, rk_mu, rk_w0, rk_w_up, rk_a0, rk_a_up, rk_g_up,
                         rk_k_k, rk_k_a, rk_r_k, rk_ln_w, rk_ln_b)
    z1, z2 = jnp.split(pd, 2, axis=-1)
    zd = causal_depthwise_conv(z1 * jax.nn.sigmoid(z2), cm_conv_w) + cm_conv_b
    y_d = jax.nn.silu(layer_norm(zd, cm_ln_w, cm_ln_b))
    return jnp.concatenate([y_a, y_b, y_c, y_d], axis=-1) @ w_out


def setup_inputs(seed: int = 0) -> dict:
    key = jax.random.key(seed)
    ks = iter(jax.random.split(key, 48))
    nrm = lambda shape, scale: jax.random.normal(next(ks), shape, jnp.float32) * scale
    gain = lambda shape: 1.0 + 0.02 * jax.random.normal(next(ks), shape, jnp.float32)
    L, D, F, G = DEPTH, D_MODEL, D_FF, GROUP_W
    return {
        "x": nrm((BATCH, SEQ, D), 1.0),
        "ffn1_pre_g": gain((L, D)),
        "ffn1_w_gate": nrm((L, D, F), D ** -0.5),
        "ffn1_w_up": nrm((L, D, F), D ** -0.5),
        "ffn1_w_down": nrm((L, F, D), F ** -0.5),
        "ffn1_post_g": gain((L, D)),
        "mix_pre_g": gain((L, D)),
        "w_in": nrm((L, D, IN_COLS), D ** -0.5),
        "sc_conv_w": nrm((L, SC_WIDTH, G), SC_WIDTH ** -0.5),
        "sg_ln_w": gain((L, G)),
        "sg_ln_b": nrm((L, G), 0.02),
        "sg_w": nrm((L, N_HEADS, CHUNK, CHUNK), 0.5 * CHUNK ** -0.5),
        "sg_b": gain((L, N_HEADS, CHUNK)) + nrm((L, N_HEADS, CHUNK), 0.1),
        "rk_mu": jax.random.uniform(next(ks), (L, C_COLS), jnp.float32, 0.0, 1.0),
        "rk_w0": jax.random.uniform(next(ks), (L, G), jnp.float32, -6.0, 1.0),
        "rk_w_up": nrm((L, RANK_W, G), 0.1 * RANK_W ** -0.5),
        "rk_a0": nrm((L, G), 0.1),
        "rk_a_up": nrm((L, RANK_A, G), 0.1 * RANK_A ** -0.5),
        "rk_g_up": nrm((L, RANK_G, G), RANK_G ** -0.5),
        "rk_k_k": 0.85 + nrm((L, G), 0.02),
        "rk_k_a": gain((L, G)),
        "rk_r_k": nrm((L, N_HEADS, HEAD_DIM), 0.1),
        "rk_ln_w": gain((L, G)),
        "rk_ln_b": nrm((L, G), 0.02),
        "cm_conv_w": nrm((L, CM_WIDTH, G), CM_WIDTH ** -0.5),
        "cm_conv_b": nrm((L, G), 0.02),
        "cm_ln_w": gain((L, G)),
        "cm_ln_b": nrm((L, G), 0.02),
        "w_out": nrm((L, D, D), D ** -0.5),
        "mix_post_g": gain((L, D)),
        "ffn2_pre_g": gain((L, D)),
        "ffn2_w_gate": nrm((L, D, F), D ** -0.5),
        "ffn2_w_up": nrm((L, D, F), D ** -0.5),
        "ffn2_w_down": nrm((L, F, D), F ** -0.5),
        "ffn2_post_g": gain((L, D)),
    }


def reference(x, ffn1_pre_g, ffn1_w_gate, ffn1_w_up, ffn1_w_down, ffn1_post_g,
              mix_pre_g, w_in, sc_conv_w, sg_ln_w, sg_ln_b, sg_w, sg_b,
              rk_mu, rk_w0, rk_w_up, rk_a0, rk_a_up, rk_g_up, rk_k_k, rk_k_a, rk_r_k,
              rk_ln_w, rk_ln_b, cm_conv_w, cm_conv_b, cm_ln_w, cm_ln_b, w_out, mix_post_g,
              ffn2_pre_g, ffn2_w_gate, ffn2_w_up, ffn2_w_down, ffn2_post_g):
    for l in range(DEPTH):
        h = rms_norm(x, ffn1_pre_g[l])
        x = x + FFN_RESID * rms_norm(
            swiglu(h, ffn1_w_gate[l], ffn1_w_up[l], ffn1_w_down[l]), ffn1_post_g[l])
        h = rms_norm(x, mix_pre_g[l])
        m = hybrid_mix(h, w_in[l], sc_conv_w[l], sg_ln_w[l], sg_ln_b[l], sg_w[l], sg_b[l],
                       rk_mu[l], rk_w0[l], rk_w_up[l], rk_a0[l], rk_a_up[l], rk_g_up[l],
                       rk_k_k[l], rk_k_a[l], rk_r_k[l], rk_ln_w[l], rk_ln_b[l],
                       cm_conv_w[l], cm_conv_b[l], cm_ln_w[l], cm_ln_b[l], w_out[l])
        x = x + rms_norm(m, mix_post_g[l])
        h = rms_norm(x, ffn2_pre_g[l])
        x = x + FFN_RESID * rms_norm(
            swiglu(h, ffn2_w_gate[l], ffn2_w_up[l], ffn2_w_down[l]), ffn2_post_g[l])
    return x
```

```python
import functools
import math

import jax
import jax.numpy as jnp
from jax import lax
from jax.experimental import pallas as pl
from jax.experimental.pallas import tpu as pltpu

F32 = jnp.float32
BF16 = jnp.bfloat16

D_MODEL = 1024
D_FF = 2816
GROUP_W = 256
HEAD_DIM = 64
N_HEADS = 4
SC_WIDTH = 3
CHUNK = 128
CM_WIDTH = 31
RANK_W = 32
RANK_A = 32
RANK_G = 64
A_COLS = 3 * GROUP_W
B_COLS = 2 * GROUP_W
C_COLS = 3 * GROUP_W + RANK_W + RANK_A + RANK_G
D_COLS = 2 * GROUP_W
IN_COLS = A_COLS + B_COLS + C_COLS + D_COLS
LORA_W = RANK_W + RANK_A + RANK_G
RMS_EPS = 1e-6
LN_EPS = 1e-5
GN_EPS = 1e-5 * HEAD_DIM
FFN_RESID = 0.5

V7X_VMEM_BYTES = 64 * 1024 * 1024
VMEM_LIMIT_BYTES = V7X_VMEM_BYTES - 8 * 1024 * 1024
SUBLANES = 8

TOKEN_TILE = 512
FFN_CHUNK = 256
RWKV_CHUNK = 64
CONV_PAD = 32
CONV_TILE = 128
PAIR_W = 2 * HEAD_DIM


def _params(*sem):
    return pltpu.CompilerParams(dimension_semantics=sem, vmem_limit_bytes=VMEM_LIMIT_BYTES)


def _resident(shape):
    nd = len(shape)
    return pl.BlockSpec(shape, lambda *_: (0,) * nd, pipeline_mode=pl.Buffered(1))


def _mm(a, b):
    return jnp.dot(a.astype(BF16), b.astype(BF16), preferred_element_type=F32)


def _mm_nt(a, b):
    return lax.dot_general(a.astype(BF16), b.astype(BF16), (((1,), (1,)), ((), ())),
                           preferred_element_type=F32)


def _mm_tn(a, b):
    return lax.dot_general(a.astype(BF16), b.astype(BF16), (((0,), (0,)), ((), ())),
                           preferred_element_type=F32)


def _rms(x, g):
    return x * lax.rsqrt(jnp.mean(x * x, axis=-1, keepdims=True) + RMS_EPS) * g


def _layer_norm(x, g, b, eps):
    mu = jnp.mean(x, axis=-1, keepdims=True)
    xc = x - mu
    var = jnp.mean(xc * xc, axis=-1, keepdims=True)
    return xc * lax.rsqrt(var + eps) * g + b


def _ffn_body(x_ref, pre_g_ref, wg_ref, wu_ref, wd_ref, post_g_ref, o_ref, acc_ref):
    x = x_ref[...]
    h = _rms(x, pre_g_ref[...]).astype(BF16)
    for c in range(D_FF // FFN_CHUNK):
        sl = slice(c * FFN_CHUNK, (c + 1) * FFN_CHUNK)
        g = jnp.dot(h, wg_ref[:, sl], preferred_element_type=F32)
        u = jnp.dot(h, wu_ref[:, sl], preferred_element_type=F32)
        act = (g * jax.nn.sigmoid(g) * u).astype(BF16)
        part = jnp.dot(act, wd_ref[sl, :], preferred_element_type=F32)
        if c == 0:
            acc_ref[...] = part
        else:
            acc_ref[...] += part
    o_ref[...] = x + FFN_RESID * _rms(acc_ref[...], post_g_ref[...])


def _ffn(x, pre_g, wg, wu, wd, post_g):
    t, d = x.shape
    f = wg.shape[1]
    row = pl.BlockSpec((TOKEN_TILE, d), lambda i: (i, 0))
    return pl.pallas_call(
        _ffn_body,
        grid=(t // TOKEN_TILE,),
        in_specs=[row, _resident((1, d)), _resident((d, f)), _resident((d, f)),
                  _resident((f, d)), _resident((1, d))],
        out_specs=row,
        out_shape=jax.ShapeDtypeStruct((t, d), F32),
        scratch_shapes=[pltpu.VMEM((TOKEN_TILE, d), F32)],
        compiler_params=_params("parallel"),
        name="ffn",
    )(x, pre_g.reshape(1, d), wg, wu, wd, post_g.reshape(1, d))


_IN_SPLITS = (A_COLS, B_COLS, C_COLS, D_COLS)


def _mix_in_body(x_ref, g_ref, w_ref, pa_ref, pb_ref, pc_ref, pd_ref):
    h = _rms(x_ref[...], g_ref[...]).astype(BF16)
    off = 0
    for width, o_ref in zip(_IN_SPLITS, (pa_ref, pb_ref, pc_ref, pd_ref)):
        o_ref[...] = jnp.dot(h, w_ref[:, off:off + width], preferred_element_type=F32)
        off += width


def _mix_in(x, g, w_in):
    t, d = x.shape
    return pl.pallas_call(
        _mix_in_body,
        grid=(t // TOKEN_TILE,),
        in_specs=[pl.BlockSpec((TOKEN_TILE, d), lambda i: (i, 0)), _resident((1, d)),
                  _resident((d, IN_COLS))],
        out_specs=[pl.BlockSpec((TOKEN_TILE, w), lambda i: (i, 0)) for w in _IN_SPLITS],
        out_shape=[jax.ShapeDtypeStruct((t, w), F32) for w in _IN_SPLITS],
        compiler_params=_params("parallel"),
        name="mix_in",
    )(x, g.reshape(1, d), w_in)


def _causal_conv(zpad_ref, w_ref, width, emit):
    seq = zpad_ref.shape[0] - CONV_PAD

    def tile(i, carry):
        t0 = pl.multiple_of(i * CONV_TILE, CONV_TILE)
        zh = zpad_ref[pl.ds(t0, CONV_TILE + CONV_PAD), :]
        acc = None
        for r in range(min(SUBLANES, width)):
            zr = zh if r == 0 else pltpu.roll(zh, r, 0)
            for q in range((width - 1 - r) // SUBLANES + 1):
                j = width - 1 - (SUBLANES * q + r)
                start = CONV_PAD - SUBLANES * q
                term = zr[start:start + CONV_TILE] * w_ref[j:j + 1, :]
                acc = term if acc is None else acc + term
        emit(t0, acc)
        return carry

    lax.fori_loop(0, seq // CONV_TILE, tile, 0)


def _mix_a_body(pa_ref, w_ref, o_ref, zpad_ref):
    g = GROUP_W
    zpad_ref[0:CONV_PAD, :] = jnp.zeros((CONV_PAD, g), F32)
    zpad_ref[CONV_PAD:, :] = pa_ref[0, :, g:2 * g] * pa_ref[0, :, 2 * g:3 * g]

    def emit(t0, conv):
        o_ref[0, pl.ds(t0, CONV_TILE), :] = (pa_ref[0, pl.ds(t0, CONV_TILE), 0:g] * conv).astype(BF16)

    _causal_conv(zpad_ref, w_ref, SC_WIDTH, emit)


def _mix_a(pa, conv_w):
    b, s, _ = pa.shape
    return pl.pallas_call(
        _mix_a_body,
        grid=(b,),
        in_specs=[pl.BlockSpec((1, s, A_COLS), lambda i: (i, 0, 0)), _resident((SC_WIDTH, GROUP_W))],
        out_specs=pl.BlockSpec((1, s, GROUP_W), lambda i: (i, 0, 0)),
        out_shape=jax.ShapeDtypeStruct((b, s, GROUP_W), BF16),
        scratch_shapes=[pltpu.VMEM((s + CONV_PAD, GROUP_W), F32)],
        compiler_params=_params("parallel"),
        name="mix_a",
    )(pa, conv_w)


def _mix_b_body(pb_ref, lnw_ref, lnb_ref, w_ref, bias_ref, o_ref):
    g = GROUP_W
    seq = pb_ref.shape[1]
    row = lax.broadcasted_iota(jnp.int32, (CHUNK, CHUNK), 0)
    col = lax.broadcasted_iota(jnp.int32, (CHUNK, CHUNK), 1)
    w_tril = [jnp.where(row >= col, w_ref[h], 0.0).astype(BF16) for h in range(N_HEADS)]
    lane_head = lax.broadcasted_iota(jnp.int32, (CHUNK, g), 1) // HEAD_DIM

    def chunk(n, carry):
        t0 = pl.multiple_of(n * CHUNK, CHUNK)
        v = _layer_norm(pb_ref[0, pl.ds(t0, CHUNK), g:2 * g], lnw_ref[...], lnb_ref[...], LN_EPS)
        v = v.astype(BF16)
        s = bias_ref[...]
        for h in range(N_HEADS):
            sh = jnp.dot(w_tril[h], v, preferred_element_type=F32)
            s = s + jnp.where(lane_head == h, sh, 0.0)
        o_ref[0, pl.ds(t0, CHUNK), :] = (pb_ref[0, pl.ds(t0, CHUNK), 0:g] * s).astype(BF16)
        return carry

    lax.fori_loop(0, seq // CHUNK, chunk, 0)


def _mix_b(pb, ln_w, ln_b, sg_w, sg_b):
    b, s, _ = pb.shape
    g = GROUP_W
    bias = jnp.repeat(sg_b.T, HEAD_DIM, axis=1)
    return pl.pallas_call(
        _mix_b_body,
        grid=(b,),
        in_specs=[pl.BlockSpec((1, s, B_COLS), lambda i: (i, 0, 0)), _resident((1, g)), _resident((1, g)),
                  _resident((N_HEADS, CHUNK, CHUNK)), _resident((CHUNK, g))],
        out_specs=pl.BlockSpec((1, s, g), lambda i: (i, 0, 0)),
        out_shape=jax.ShapeDtypeStruct((b, s, g), BF16),
        compiler_params=_params("parallel"),
        name="mix_b",
    )(pb, ln_w.reshape(1, g), ln_b.reshape(1, g), sg_w, bias)


def _mix_d_body(pd_ref, w_ref, cb_ref, lnw_ref, lnb_ref, o_ref, zpad_ref):
    g = GROUP_W
    zpad_ref[0:CONV_PAD, :] = jnp.zeros((CONV_PAD, g), F32)
    zpad_ref[CONV_PAD:, :] = pd_ref[0, :, 0:g] * jax.nn.sigmoid(pd_ref[0, :, g:2 * g])

    def emit(t0, conv):
        y = _layer_norm(conv + cb_ref[...], lnw_ref[...], lnb_ref[...], LN_EPS)
        o_ref[0, pl.ds(t0, CONV_TILE), :] = (y * jax.nn.sigmoid(y)).astype(BF16)

    _causal_conv(zpad_ref, w_ref, CM_WIDTH, emit)


def _mix_d(pd, conv_w, conv_b, ln_w, ln_b):
    b, s, _ = pd.shape
    g = GROUP_W
    return pl.pallas_call(
        _mix_d_body,
        grid=(b,),
        in_specs=[pl.BlockSpec((1, s, D_COLS), lambda i: (i, 0, 0)), _resident((CM_WIDTH, g)),
                  _resident((1, g)), _resident((1, g)), _resident((1, g))],
        out_specs=pl.BlockSpec((1, s, g), lambda i: (i, 0, 0)),
        out_shape=jax.ShapeDtypeStruct((b, s, g), BF16),
        scratch_shapes=[pltpu.VMEM((s + CONV_PAD, g), F32)],
        compiler_params=_params("parallel"),
        name="mix_d",
    )(pd, conv_w, conv_b.reshape(1, g), ln_w.reshape(1, g), ln_b.reshape(1, g))


def _split2(x):
    hi = x.astype(BF16)
    lo = (x - hi.astype(F32)).astype(BF16)
    return hi, lo


def _head_sum(x, ones_bd):
    hi, lo = _split2(x)
    return (jnp.dot(hi, ones_bd, preferred_element_type=F32)
            + jnp.dot(lo, ones_bd, preferred_element_type=F32))


def _rwkv_body(pc_ref, mu_ref, w0_ref, lora_ref, a0_ref, kk_ref, ka_ref, rk_ref, lnw_ref, lnb_ref,
               o_ref, state_ref, prev_ref):
    g = GROUP_W
    lc = RWKV_CHUNK
    seq = pc_ref.shape[1]

    row = lax.broadcasted_iota(jnp.int32, (lc, lc), 0)
    col = lax.broadcasted_iota(jnp.int32, (lc, lc), 1)
    strict = row > col
    incl = row >= col
    tri_ones = jnp.where(incl, 1.0, 0.0).astype(BF16)
    bd_r = lax.broadcasted_iota(jnp.int32, (g, g), 0) // HEAD_DIM
    bd_c = lax.broadcasted_iota(jnp.int32, (g, g), 1) // HEAD_DIM
    ones_bd = jnp.where(bd_r == bd_c, 1.0, 0.0).astype(BF16)
    pr = lax.broadcasted_iota(jnp.int32, (PAIR_W, PAIR_W), 0)
    pcol = lax.broadcasted_iota(jnp.int32, (PAIR_W, PAIR_W), 1)
    pair_bd = (pr // HEAD_DIM) == (pcol // HEAD_DIM)
    pair_eye = pr == pcol
    first_head = lax.broadcasted_iota(jnp.int32, (lc, PAIR_W), 1) < HEAD_DIM
    first_head2 = jnp.concatenate([first_head, first_head], axis=1)
    row0 = lax.broadcasted_iota(jnp.int32, (lc, C_COLS), 0) == 0

    state_ref[...] = jnp.zeros(state_ref.shape, F32)
    prev_ref[...] = jnp.zeros(prev_ref.shape, F32)

    def chunk(c, carry):
        t0 = pl.multiple_of(c * lc, lc)
        x = pc_ref[0, pl.ds(t0, lc), :]
        xs = jnp.where(row0, prev_ref[...], pltpu.roll(x, 1, 0))
        prev_ref[...] = x[lc - 1:lc, :]
        xm = x + (xs - x) * mu_ref[...]
        r = xm[:, 0:g]
        k = xm[:, g:2 * g]
        v = xm[:, 2 * g:3 * g]
        lo_in = xm[:, 3 * g:3 * g + LORA_W]
        lane = lax.broadcasted_iota(jnp.int32, (lc, LORA_W), 1)
        lo_act = jnp.where(lane < RANK_W, jnp.tanh(lo_in),
                           jnp.where(lane < RANK_W + RANK_A, lo_in, jax.nn.sigmoid(lo_in)))
        lora = _mm(lo_act, lora_ref[...])
        e = jax.nn.sigmoid(w0_ref[...] + lora[:, 0:g]) * math.exp(-0.5)
        a_sig = jax.nn.sigmoid(a0_ref[...] + lora[:, g:2 * g])
        gate = lora[:, 2 * g:3 * g]
        kk = k * kk_ref[...]
        kk = kk / jnp.maximum(jnp.sqrt(_head_sum(kk * kk, ones_bd)), 1e-12)
        k = k * (1.0 + (a_sig - 1.0) * ka_ref[...])
        a_ = -kk
        b_ = kk * a_sig
        bonus = _head_sum(r * k * rk_ref[...], ones_bd) * v

        e_hi = e.astype(BF16)
        e_r = e - e_hi.astype(F32)
        e_mid = e_r.astype(BF16)
        e_lo = (e_r - e_mid.astype(F32)).astype(BF16)
        cs = (jnp.dot(tri_ones, e_hi, preferred_element_type=F32)
              + jnp.dot(tri_ones, e_mid, preferred_element_type=F32)
              + jnp.dot(tri_ones, e_lo, preferred_element_type=F32))
        cs_last = cs[lc - 1:lc, :]
        w_inc = jnp.exp(-cs)
        w_exc = jnp.exp(e - cs)
        w_inv = jnp.exp(cs)
        w_fin = jnp.exp(cs - cs_last)
        w_chunk = jnp.exp(-cs_last)
        at_all = a_ * w_exc
        rt_all = r * w_inc
        bt_all = b_ * w_inv
        kt_all = k * w_inv
        bh_all = b_ * w_fin
        kh_all = k * w_fin

        outs = []
        for p in range(N_HEADS // 2):
            ps = slice(p * PAIR_W, (p + 1) * PAIR_W)
            at, rt, bt, kt, bh, kh, vp = (t[:, ps] for t in
                                          (at_all, rt_all, bt_all, kt_all, bh_all, kh_all, v))
            a_ab, a_rb, a_rk, q0 = [], [], [], []
            for hm in (first_head, ~first_head):
                lhs = jnp.concatenate([jnp.where(hm, at, 0.0), jnp.where(hm, rt, 0.0)], axis=0)
                ab = _mm_nt(lhs, bt)
                ak = _mm_nt(lhs, kt)
                a_ab.append(jnp.where(strict, ab[0:lc], 0.0))
                a_rb.append(jnp.where(incl, ab[lc:], 0.0))
                a_rk.append(jnp.where(incl, ak[lc:], 0.0))
                q0.append(_mm(jnp.where(strict, ak[0:lc], 0.0), vp))
            z0 = jnp.concatenate([at, jnp.where(first_head, q0[0], q0[1])], axis=1)
            zs = []
            for h in range(2):
                z, ap = z0, a_ab[h]
                n_fac = int(math.log2(lc))
                for i in range(n_fac):
                    z = z + _mm(ap, z)
                    if i + 1 < n_fac:
                        ap = _mm(ap, ap)
                zs.append(z)
            z = jnp.where(first_head2, zs[0], zs[1])
            gy = [_mm(a_rb[h], z) for h in range(2)]
            yk = [_mm(a_rk[h], vp) for h in range(2)]
            g_mat = rt + jnp.where(first_head, gy[0][:, 0:PAIR_W], gy[1][:, 0:PAIR_W])
            y0 = jnp.where(first_head, gy[0][:, PAIR_W:] + yk[0], gy[1][:, PAIR_W:] + yk[1])
            mc = _mm_tn(bh, z)
            kv = _mm_tn(kh, vp)
            m_mat = jnp.where(pair_bd, mc[:, 0:PAIR_W], 0.0) + jnp.where(pair_eye, w_chunk[:, ps], 0.0)
            c_mat = jnp.where(pair_bd, mc[:, PAIR_W:] + kv, 0.0)
            h0 = state_ref[p]
            step = _mm(jnp.concatenate([g_mat, m_mat], axis=0), h0)
            outs.append(step[0:lc] + y0)
            state_ref[p] = step[lc:] + c_mat

        o = jnp.concatenate(outs, axis=1)
        mean = _head_sum(o, ones_bd) * (1.0 / HEAD_DIM)
        oc = o - mean
        var = _head_sum(oc * oc, ones_bd) * (1.0 / HEAD_DIM)
        o = oc * lax.rsqrt(var + GN_EPS) * lnw_ref[...] + lnb_ref[...]
        o_ref[0, pl.ds(t0, lc), :] = ((o + bonus) * gate).astype(BF16)
        return carry

    lax.fori_loop(0, seq // lc, chunk, 0)


def _mix_c(pc, mu, w0, w_up, a0, a_up, g_up, k_k, k_a, r_k, ln_w, ln_b):
    b, s, _ = pc.shape
    g = GROUP_W
    lora = jnp.zeros((LORA_W, 3 * g), F32)
    lora = lora.at[0:RANK_W, 0:g].set(w_up)
    lora = lora.at[RANK_W:RANK_W + RANK_A, g:2 * g].set(a_up)
    lora = lora.at[RANK_W + RANK_A:, 2 * g:].set(g_up)
    vec = lambda t: t.reshape(1, g)
    return pl.pallas_call(
        _rwkv_body,
        grid=(b,),
        in_specs=[pl.BlockSpec((1, s, C_COLS), lambda i: (i, 0, 0)), _resident((1, C_COLS)),
                  _resident((1, g)), _resident((LORA_W, 3 * g)), _resident((1, g)), _resident((1, g)),
                  _resident((1, g)), _resident((1, g)), _resident((1, g)), _resident((1, g))],
        out_specs=pl.BlockSpec((1, s, g), lambda i: (i, 0, 0)),
        out_shape=jax.ShapeDtypeStruct((b, s, g), BF16),
        scratch_shapes=[pltpu.VMEM((N_HEADS // 2, PAIR_W, PAIR_W), F32), pltpu.VMEM((1, C_COLS), F32)],
        compiler_params=_params("parallel"),
        name="mix_c",
    )(pc, mu.reshape(1, C_COLS), vec(w0), lora.astype(BF16), vec(a0), vec(k_k), vec(k_a), vec(r_k),
      vec(ln_w), vec(ln_b))


def _mix_out_body(x_ref, ya_ref, yb_ref, yc_ref, yd_ref, w_ref, g_ref, o_ref):
    m = None
    for i, y_ref in enumerate((ya_ref, yb_ref, yc_ref, yd_ref)):
        part = jnp.dot(y_ref[...], w_ref[i * GROUP_W:(i + 1) * GROUP_W, :], preferred_element_type=F32)
        m = part if m is None else m + part
    o_ref[...] = x_ref[...] + _rms(m, g_ref[...])


def _mix_out(x, ys, w_out, g):
    t, d = x.shape
    row = pl.BlockSpec((TOKEN_TILE, d), lambda i: (i, 0))
    yspec = pl.BlockSpec((TOKEN_TILE, GROUP_W), lambda i: (i, 0))
    return pl.pallas_call(
        _mix_out_body,
        grid=(t // TOKEN_TILE,),
        in_specs=[row, yspec, yspec, yspec, yspec, _resident((d, d)), _resident((1, d))],
        out_specs=row,
        out_shape=jax.ShapeDtypeStruct((t, d), F32),
        compiler_params=_params("parallel"),
        name="mix_out",
    )(x, *ys, w_out, g.reshape(1, d))


def kernel(x, ffn1_pre_g, ffn1_w_gate, ffn1_w_up, ffn1_w_down, ffn1_post_g, mix_pre_g, w_in, sc_conv_w, sg_ln_w, sg_ln_b, sg_w, sg_b, rk_mu, rk_w0, rk_w_up, rk_a0, rk_a_up, rk_g_up, rk_k_k, rk_k_a, rk_r_k, rk_ln_w, rk_ln_b, cm_conv_w, cm_conv_b, cm_ln_w, cm_ln_b, w_out, mix_post_g, ffn2_pre_g, ffn2_w_gate, ffn2_w_up, ffn2_w_down, ffn2_post_g):
    b, s, d = x.shape
    t = b * s
    bf = lambda w: w.astype(BF16)
    xf = x.reshape(t, d)
    for l in range(ffn1_pre_g.shape[0]):
        xf = _ffn(xf, ffn1_pre_g[l], bf(ffn1_w_gate[l]), bf(ffn1_w_up[l]), bf(ffn1_w_down[l]),
                  ffn1_post_g[l])
        pa, pb, pc, pd = _mix_in(xf, mix_pre_g[l], bf(w_in[l]))
        seq = lambda p: p.reshape(b, s, p.shape[-1])
        ya = _mix_a(seq(pa), sc_conv_w[l])
        yb = _mix_b(seq(pb), sg_ln_w[l], sg_ln_b[l], sg_w[l], sg_b[l])
        yc = _mix_c(seq(pc), rk_mu[l], rk_w0[l], rk_w_up[l], rk_a0[l], rk_a_up[l], rk_g_up[l],
                    rk_k_k[l], rk_k_a[l], rk_r_k[l].reshape(-1), rk_ln_w[l], rk_ln_b[l])
        yd = _mix_d(seq(pd), cm_conv_w[l], cm_conv_b[l], cm_ln_w[l], cm_ln_b[l])
        flat = lambda y: y.reshape(t, GROUP_W)
        xf = _mix_out(xf, (flat(ya), flat(yb), flat(yc), flat(yd)), bf(w_out[l]), mix_post_g[l])
        xf = _ffn(xf, ffn2_pre_g[l], bf(ffn2_w_gate[l]), bf(ffn2_w_up[l]), bf(ffn2_w_down[l]),
                  ffn2_post_g[l])
    return xf.reshape(b, s, d)
```

```python
import functools
import math

import jax
import jax.numpy as jnp
from jax import lax
from jax.experimental import pallas as pl
from jax.experimental.pallas import tpu as pltpu

F32 = jnp.float32
BF16 = jnp.bfloat16

D_MODEL = 1024
D_FF = 2816
GROUP_W = 256
HEAD_DIM = 64
N_HEADS = 4
SC_WIDTH = 3
CHUNK = 128
CM_WIDTH = 31
RANK_W = 32
RANK_A = 32
RANK_G = 64
A_COLS = 3 * GROUP_W
B_COLS = 2 * GROUP_W
C_COLS = 3 * GROUP_W + RANK_W + RANK_A + RANK_G
D_COLS = 2 * GROUP_W
IN_COLS = A_COLS + B_COLS + C_COLS + D_COLS
LORA_W = RANK_W + RANK_A + RANK_G
RMS_EPS = 1e-6
LN_EPS = 1e-5
GN_EPS = 1e-5 * HEAD_DIM
FFN_RESID = 0.5

V7X_VMEM_BYTES = 64 * 1024 * 1024
VMEM_LIMIT_BYTES = V7X_VMEM_BYTES - 8 * 1024 * 1024
SUBLANES = 8

TOKEN_TILE = 512
FFN_CHUNK = 256
RWKV_CHUNK = 64
RWKV_BLOCK = 4
CONV_PAD = 32
CONV_TILE = 128
PAIR_W = 2 * HEAD_DIM


def _params(*sem):
    return pltpu.CompilerParams(dimension_semantics=sem, vmem_limit_bytes=VMEM_LIMIT_BYTES)


def _resident(shape):
    nd = len(shape)
    return pl.BlockSpec(shape, lambda *_: (0,) * nd, pipeline_mode=pl.Buffered(1))


def _mm(a, b):
    return jnp.dot(a.astype(BF16), b.astype(BF16), preferred_element_type=F32)


def _mm_nt(a, b):
    return lax.dot_general(a.astype(BF16), b.astype(BF16), (((1,), (1,)), ((), ())),
                           preferred_element_type=F32)


def _mm_tn(a, b):
    return lax.dot_general(a.astype(BF16), b.astype(BF16), (((0,), (0,)), ((), ())),
                           preferred_element_type=F32)


def _rms(x, g):
    return x * lax.rsqrt(jnp.mean(x * x, axis=-1, keepdims=True) + RMS_EPS) * g


def _layer_norm(x, g, b, eps):
    mu = jnp.mean(x, axis=-1, keepdims=True)
    xc = x - mu
    var = jnp.mean(xc * xc, axis=-1, keepdims=True)
    return xc * lax.rsqrt(var + eps) * g + b


def _ffn_body(x_ref, pre_g_ref, wg_ref, wu_ref, wd_ref, post_g_ref, o_ref, acc_ref):
    x = x_ref[...]
    h = _rms(x, pre_g_ref[...]).astype(BF16)
    for c in range(D_FF // FFN_CHUNK):
        sl = slice(c * FFN_CHUNK, (c + 1) * FFN_CHUNK)
        g = jnp.dot(h, wg_ref[:, sl], preferred_element_type=F32)
        u = jnp.dot(h, wu_ref[:, sl], preferred_element_type=F32)
        act = (g * jax.nn.sigmoid(g) * u).astype(BF16)
        part = jnp.dot(act, wd_ref[sl, :], preferred_element_type=F32)
        if c == 0:
            acc_ref[...] = part
        else:
            acc_ref[...] += part
    o_ref[...] = x + FFN_RESID * _rms(acc_ref[...], post_g_ref[...])


def _ffn(x, pre_g, wg, wu, wd, post_g):
    t, d = x.shape
    f = wg.shape[1]
    row = pl.BlockSpec((TOKEN_TILE, d), lambda i: (i, 0))
    return pl.pallas_call(
        _ffn_body,
        grid=(t // TOKEN_TILE,),
        in_specs=[row, _resident((1, d)), _resident((d, f)), _resident((d, f)),
                  _resident((f, d)), _resident((1, d))],
        out_specs=row,
        out_shape=jax.ShapeDtypeStruct((t, d), F32),
        scratch_shapes=[pltpu.VMEM((TOKEN_TILE, d), F32)],
        compiler_params=_params("parallel"),
        name="ffn",
    )(x, pre_g.reshape(1, d), wg, wu, wd, post_g.reshape(1, d))


_IN_SPLITS = (A_COLS, B_COLS, C_COLS, D_COLS)


def _mix_in_body(x_ref, g_ref, w_ref, pa_ref, pb_ref, pc_ref, pd_ref):
    h = _rms(x_ref[...], g_ref[...]).astype(BF16)
    off = 0
    for width, o_ref in zip(_IN_SPLITS, (pa_ref, pb_ref, pc_ref, pd_ref)):
        o_ref[...] = jnp.dot(h, w_ref[:, off:off + width], preferred_element_type=F32)
        off += width


def _mix_in(x, g, w_in):
    t, d = x.shape
    return pl.pallas_call(
        _mix_in_body,
        grid=(t // TOKEN_TILE,),
        in_specs=[pl.BlockSpec((TOKEN_TILE, d), lambda i: (i, 0)), _resident((1, d)),
                  _resident((d, IN_COLS))],
        out_specs=[pl.BlockSpec((TOKEN_TILE, w), lambda i: (i, 0)) for w in _IN_SPLITS],
        out_shape=[jax.ShapeDtypeStruct((t, w), F32) for w in _IN_SPLITS],
        compiler_params=_params("parallel"),
        name="mix_in",
    )(x, g.reshape(1, d), w_in)


def _causal_conv(zpad_ref, w_ref, width, emit):
    seq = zpad_ref.shape[0] - CONV_PAD

    def tile(i, carry):
        t0 = pl.multiple_of(i * CONV_TILE, CONV_TILE)
        zh = zpad_ref[pl.ds(t0, CONV_TILE + CONV_PAD), :]
        acc = None
        for r in range(min(SUBLANES, width)):
            zr = zh if r == 0 else pltpu.roll(zh, r, 0)
            for q in range((width - 1 - r) // SUBLANES + 1):
                j = width - 1 - (SUBLANES * q + r)
                start = CONV_PAD - SUBLANES * q
                term = zr[start:start + CONV_TILE] * w_ref[j:j + 1, :]
                acc = term if acc is None else acc + term
        emit(t0, acc)
        return carry

    lax.fori_loop(0, seq // CONV_TILE, tile, 0)


def _mix_a_body(pa_ref, w_ref, o_ref, zpad_ref):
    g = GROUP_W
    zpad_ref[0:CONV_PAD, :] = jnp.zeros((CONV_PAD, g), F32)
    zpad_ref[CONV_PAD:, :] = pa_ref[0, :, g:2 * g] * pa_ref[0, :, 2 * g:3 * g]

    def emit(t0, conv):
        o_ref[0, pl.ds(t0, CONV_TILE), :] = (pa_ref[0, pl.ds(t0, CONV_TILE), 0:g] * conv).astype(BF16)

    _causal_conv(zpad_ref, w_ref, SC_WIDTH, emit)


def _mix_a(pa, conv_w):
    b, s, _ = pa.shape
    return pl.pallas_call(
        _mix_a_body,
        grid=(b,),
        in_specs=[pl.BlockSpec((1, s, A_COLS), lambda i: (i, 0, 0)), _resident((SC_WIDTH, GROUP_W))],
        out_specs=pl.BlockSpec((1, s, GROUP_W), lambda i: (i, 0, 0)),
        out_shape=jax.ShapeDtypeStruct((b, s, GROUP_W), BF16),
        scratch_shapes=[pltpu.VMEM((s + CONV_PAD, GROUP_W), F32)],
        compiler_params=_params("parallel"),
        name="mix_a",
    )(pa, conv_w)


def _mix_b_body(pb_ref, lnw_ref, lnb_ref, w_ref, bias_ref, o_ref):
    g = GROUP_W
    seq = pb_ref.shape[1]
    row = lax.broadcasted_iota(jnp.int32, (CHUNK, CHUNK), 0)
    col = lax.broadcasted_iota(jnp.int32, (CHUNK, CHUNK), 1)
    w_tril = [jnp.where(row >= col, w_ref[h], 0.0).astype(BF16) for h in range(N_HEADS)]
    lane_head = lax.broadcasted_iota(jnp.int32, (CHUNK, g), 1) // HEAD_DIM

    def chunk(n, carry):
        t0 = pl.multiple_of(n * CHUNK, CHUNK)
        v = _layer_norm(pb_ref[0, pl.ds(t0, CHUNK), g:2 * g], lnw_ref[...], lnb_ref[...], LN_EPS)
        v = v.astype(BF16)
        s = bias_ref[...]
        for h in range(N_HEADS):
            sh = jnp.dot(w_tril[h], v, preferred_element_type=F32)
            s = s + jnp.where(lane_head == h, sh, 0.0)
        o_ref[0, pl.ds(t0, CHUNK), :] = (pb_ref[0, pl.ds(t0, CHUNK), 0:g] * s).astype(BF16)
        return carry

    lax.fori_loop(0, seq // CHUNK, chunk, 0)


def _mix_b(pb, ln_w, ln_b, sg_w, sg_b):
    b, s, _ = pb.shape
    g = GROUP_W
    bias = jnp.repeat(sg_b.T, HEAD_DIM, axis=1)
    return pl.pallas_call(
        _mix_b_body,
        grid=(b,),
        in_specs=[pl.BlockSpec((1, s, B_COLS), lambda i: (i, 0, 0)), _resident((1, g)), _resident((1, g)),
                  _resident((N_HEADS, CHUNK, CHUNK)), _resident((CHUNK, g))],
        out_specs=pl.BlockSpec((1, s, g), lambda i: (i, 0, 0)),
        out_shape=jax.ShapeDtypeStruct((b, s, g), BF16),
        compiler_params=_params("parallel"),
        name="mix_b",
    )(pb, ln_w.reshape(1, g), ln_b.reshape(1, g), sg_w, bias)


def _mix_d_body(pd_ref, w_ref, cb_ref, lnw_ref, lnb_ref, o_ref, zpad_ref):
    g = GROUP_W
    zpad_ref[0:CONV_PAD, :] = jnp.zeros((CONV_PAD, g), F32)
    zpad_ref[CONV_PAD:, :] = pd_ref[0, :, 0:g] * jax.nn.sigmoid(pd_ref[0, :, g:2 * g])

    def emit(t0, conv):
        y = _layer_norm(conv + cb_ref[...], lnw_ref[...], lnb_ref[...], LN_EPS)
        o_ref[0, pl.ds(t0, CONV_TILE), :] = (y * jax.nn.sigmoid(y)).astype(BF16)

    _causal_conv(zpad_ref, w_ref, CM_WIDTH, emit)


def _mix_d(pd, conv_w, conv_b, ln_w, ln_b):
    b, s, _ = pd.shape
    g = GROUP_W
    return pl.pallas_call(
        _mix_d_body,
        grid=(b,),
        in_specs=[pl.BlockSpec((1, s, D_COLS), lambda i: (i, 0, 0)), _resident((CM_WIDTH, g)),
                  _resident((1, g)), _resident((1, g)), _resident((1, g))],
        out_specs=pl.BlockSpec((1, s, g), lambda i: (i, 0, 0)),
        out_shape=jax.ShapeDtypeStruct((b, s, g), BF16),
        scratch_shapes=[pltpu.VMEM((s + CONV_PAD, g), F32)],
        compiler_params=_params("parallel"),
        name="mix_d",
    )(pd, conv_w, conv_b.reshape(1, g), ln_w.reshape(1, g), ln_b.reshape(1, g))


def _split2(x):
    hi = x.astype(BF16)
    lo = (x - hi.astype(F32)).astype(BF16)
    return hi, lo


def _head_sum(x, ones_bd):
    hi, lo = _split2(x)
    return (jnp.dot(hi, ones_bd, preferred_element_type=F32)
            + jnp.dot(lo, ones_bd, preferred_element_type=F32))


def _rwkv_body(pc_ref, mu_ref, w0_ref, lora_ref, a0_ref, kk_ref, ka_ref, rk_ref, lnw_ref, lnb_ref,
               o_ref, gm_ref, yc_ref, rec_ref, bonus_ref, gate_ref, state_ref, prev_ref):
    g = GROUP_W
    lc = RWKV_CHUNK
    tb = RWKV_BLOCK * lc
    seq = pc_ref.shape[1]

    row = lax.broadcasted_iota(jnp.int32, (lc, lc), 0)
    col = lax.broadcasted_iota(jnp.int32, (lc, lc), 1)
    strict = row > col
    incl = row >= col
    brow = lax.broadcasted_iota(jnp.int32, (tb, tb), 0)
    bcol = lax.broadcasted_iota(jnp.int32, (tb, tb), 1)
    tri_ones = jnp.where((brow // lc == bcol // lc) & (brow >= bcol), 1.0, 0.0).astype(BF16)
    bd_r = lax.broadcasted_iota(jnp.int32, (g, g), 0) // HEAD_DIM
    bd_c = lax.broadcasted_iota(jnp.int32, (g, g), 1) // HEAD_DIM
    ones_bd = jnp.where(bd_r == bd_c, 1.0, 0.0).astype(BF16)
    pr = lax.broadcasted_iota(jnp.int32, (PAIR_W, PAIR_W), 0)
    pcol = lax.broadcasted_iota(jnp.int32, (PAIR_W, PAIR_W), 1)
    pair_bd = (pr // HEAD_DIM) == (pcol // HEAD_DIM)
    pair_eye = pr == pcol
    first_head = lax.broadcasted_iota(jnp.int32, (lc, PAIR_W), 1) < HEAD_DIM
    first_head2 = jnp.concatenate([first_head, first_head], axis=1)
    row0 = lax.broadcasted_iota(jnp.int32, (tb, C_COLS), 0) == 0

    state_ref[...] = jnp.zeros(state_ref.shape, F32)
    prev_ref[...] = jnp.zeros(prev_ref.shape, F32)

    def prepare(i, carry):
        t0 = pl.multiple_of(i * tb, tb)
        x = pc_ref[0, pl.ds(t0, tb), :]
        xs = jnp.where(row0, prev_ref[...], pltpu.roll(x, 1, 0))
        prev_ref[...] = x[tb - 1:tb, :]
        xm = x + (xs - x) * mu_ref[...]
        r = xm[:, 0:g]
        k = xm[:, g:2 * g]
        v = xm[:, 2 * g:3 * g]
        lo_in = xm[:, 3 * g:3 * g + LORA_W]
        lane = lax.broadcasted_iota(jnp.int32, (tb, LORA_W), 1)
        lo_act = jnp.where(lane < RANK_W, jnp.tanh(lo_in),
                           jnp.where(lane < RANK_W + RANK_A, lo_in, jax.nn.sigmoid(lo_in)))
        lora = _mm(lo_act, lora_ref[...])
        e = jax.nn.sigmoid(w0_ref[...] + lora[:, 0:g]) * math.exp(-0.5)
        a_sig = jax.nn.sigmoid(a0_ref[...] + lora[:, g:2 * g])
        gate_ref[pl.ds(t0, tb), :] = lora[:, 2 * g:3 * g]
        kk = k * kk_ref[...]
        kk = kk / jnp.maximum(jnp.sqrt(_head_sum(kk * kk, ones_bd)), 1e-12)
        k = k * (1.0 + (a_sig - 1.0) * ka_ref[...])
        a_ = -kk
        b_ = kk * a_sig
        bonus_ref[pl.ds(t0, tb), :] = _head_sum(r * k * rk_ref[...], ones_bd) * v

        e_hi = e.astype(BF16)
        e_r = e - e_hi.astype(F32)
        e_mid = e_r.astype(BF16)
        e_lo = (e_r - e_mid.astype(F32)).astype(BF16)
        cs = (jnp.dot(tri_ones, e_hi, preferred_element_type=F32)
              + jnp.dot(tri_ones, e_mid, preferred_element_type=F32)
              + jnp.dot(tri_ones, e_lo, preferred_element_type=F32))
        cs_last = jnp.concatenate(
            [jnp.broadcast_to(cs[(ci + 1) * lc - 1:(ci + 1) * lc, :], (lc, g)) for ci in range(RWKV_BLOCK)],
            axis=0)
        w_inc = jnp.exp(-cs)
        w_exc = jnp.exp(e - cs)
        w_inv = jnp.exp(cs)
        w_fin = jnp.exp(cs - cs_last)
        w_chunk_all = jnp.exp(-cs_last)
        at_all = a_ * w_exc
        rt_all = r * w_inc
        bt_all = b_ * w_inv
        kt_all = k * w_inv
        bh_all = b_ * w_fin
        kh_all = k * w_fin

        probs = [(ci, p) for ci in range(RWKV_BLOCK) for p in range(N_HEADS // 2)]
        heads = (first_head, ~first_head)
        ops = []
        for ci, p in probs:
            rs = slice(ci * lc, (ci + 1) * lc)
            ps = slice(p * PAIR_W, (p + 1) * PAIR_W)
            ops.append(tuple(t[rs, ps] for t in (at_all, rt_all, bt_all, kt_all, bh_all, kh_all, v)))
        a_ab, a_ak, a_rb, a_rk = [], [], [], []
        for at, rt, bt, kt, bh, kh, vp in ops:
            for hm in heads:
                lhs = jnp.concatenate([jnp.where(hm, at, 0.0), jnp.where(hm, rt, 0.0)], axis=0)
                ab = _mm_nt(lhs, bt)
                ak = _mm_nt(lhs, kt)
                a_ab.append(jnp.where(strict, ab[0:lc], 0.0))
                a_ak.append(jnp.where(strict, ak[0:lc], 0.0))
                a_rb.append(jnp.where(incl, ab[lc:], 0.0))
                a_rk.append(jnp.where(incl, ak[lc:], 0.0))
        zs = []
        for n, (at, rt, bt, kt, bh, kh, vp) in enumerate(ops):
            q0 = [_mm(a_ak[2 * n + h], vp) for h in range(2)]
            z0 = jnp.concatenate([at, jnp.where(first_head, q0[0], q0[1])], axis=1)
            zs += [z0, z0]
        aps = list(a_ab)
        n_fac = int(math.log2(lc))
        for f in range(n_fac):
            zs = [z + _mm(ap, z) for ap, z in zip(aps, zs)]
            if f + 1 < n_fac:
                aps = [_mm(ap, ap) for ap in aps]
        for n, ((ci, p), (at, rt, bt, kt, bh, kh, vp)) in enumerate(zip(probs, ops)):
            ps = slice(p * PAIR_W, (p + 1) * PAIR_W)
            idx = i * RWKV_BLOCK + ci
            z = jnp.where(first_head2, zs[2 * n], zs[2 * n + 1])
            gy = [_mm(a_rb[2 * n + h], z) for h in range(2)]
            yk = [_mm(a_rk[2 * n + h], vp) for h in range(2)]
            g_mat = rt + jnp.where(first_head, gy[0][:, 0:PAIR_W], gy[1][:, 0:PAIR_W])
            y0 = jnp.where(first_head, gy[0][:, PAIR_W:] + yk[0], gy[1][:, PAIR_W:] + yk[1])
            mc = _mm_tn(bh, z)
            kv = _mm_tn(kh, vp)
            w_chunk = w_chunk_all[ci * lc:ci * lc + 1, ps]
            m_mat = jnp.where(pair_bd, mc[:, 0:PAIR_W], 0.0) + jnp.where(pair_eye, w_chunk, 0.0)
            c_mat = jnp.where(pair_bd, mc[:, PAIR_W:] + kv, 0.0)
            gm_ref[idx, p, 0:lc, :] = g_mat.astype(BF16)
            gm_ref[idx, p, lc:, :] = m_mat.astype(BF16)
            yc_ref[idx, p, 0:lc, :] = y0
            yc_ref[idx, p, lc:, :] = c_mat
        return carry

    def scan(c, carry):
        t0 = pl.multiple_of(c * lc, lc)
        for p in range(N_HEADS // 2):
            step = jnp.dot(gm_ref[c, p], state_ref[p].astype(BF16), preferred_element_type=F32)
            step = step + yc_ref[c, p]
            rec_ref[pl.ds(t0, lc), p * PAIR_W:(p + 1) * PAIR_W] = step[0:lc]
            state_ref[p] = step[lc:]
        return carry

    def finish(i, carry):
        t0 = pl.multiple_of(i * tb, tb)
        o = rec_ref[pl.ds(t0, tb), :]
        mean = _head_sum(o, ones_bd) * (1.0 / HEAD_DIM)
        oc = o - mean
        var = _head_sum(oc * oc, ones_bd) * (1.0 / HEAD_DIM)
        o = oc * lax.rsqrt(var + GN_EPS) * lnw_ref[...] + lnb_ref[...]
        o = (o + bonus_ref[pl.ds(t0, tb), :]) * gate_ref[pl.ds(t0, tb), :]
        o_ref[0, pl.ds(t0, tb), :] = o.astype(BF16)
        return carry

    lax.fori_loop(0, seq // tb, prepare, 0)
    lax.fori_loop(0, seq // lc, scan, 0)
    lax.fori_loop(0, seq // tb, finish, 0)


def _mix_c(pc, mu, w0, w_up, a0, a_up, g_up, k_k, k_a, r_k, ln_w, ln_b):
    b, s, _ = pc.shape
    g = GROUP_W
    n_pairs = N_HEADS // 2
    lora = jnp.zeros((LORA_W, 3 * g), F32)
    lora = lora.at[0:RANK_W, 0:g].set(w_up)
    lora = lora.at[RANK_W:RANK_W + RANK_A, g:2 * g].set(a_up)
    lora = lora.at[RANK_W + RANK_A:, 2 * g:].set(g_up)
    vec = lambda t: t.reshape(1, g)
    return pl.pallas_call(
        _rwkv_body,
        grid=(b,),
        in_specs=[pl.BlockSpec((1, s, C_COLS), lambda i: (i, 0, 0)), _resident((1, C_COLS)),
                  _resident((1, g)), _resident((LORA_W, 3 * g)), _resident((1, g)), _resident((1, g)),
                  _resident((1, g)), _resident((1, g)), _resident((1, g)), _resident((1, g))],
        out_specs=pl.BlockSpec((1, s, g), lambda i: (i, 0, 0)),
        out_shape=jax.ShapeDtypeStruct((b, s, g), BF16),
        scratch_shapes=[
            pltpu.VMEM((s // RWKV_CHUNK, n_pairs, RWKV_CHUNK + PAIR_W, PAIR_W), BF16),
            pltpu.VMEM((s // RWKV_CHUNK, n_pairs, RWKV_CHUNK + PAIR_W, PAIR_W), F32),
            pltpu.VMEM((s, g), F32),
            pltpu.VMEM((s, g), F32),
            pltpu.VMEM((s, g), F32),
            pltpu.VMEM((n_pairs, PAIR_W, PAIR_W), F32),
            pltpu.VMEM((1, C_COLS), F32),
        ],
        compiler_params=_params("parallel"),
        name="mix_c",
    )(pc, mu.reshape(1, C_COLS), vec(w0), lora.astype(BF16), vec(a0), vec(k_k), vec(k_a), vec(r_k),
      vec(ln_w), vec(ln_b))


def _mix_out_body(x_ref, ya_ref, yb_ref, yc_ref, yd_ref, w_ref, g_ref, o_ref):
    m = None
    for i, y_ref in enumerate((ya_ref, yb_ref, yc_ref, yd_ref)):
        part = jnp.dot(y_ref[...], w_ref[i * GROUP_W:(i + 1) * GROUP_W, :], preferred_element_type=F32)
        m = part if m is None else m + part
    o_ref[...] = x_ref[...] + _rms(m, g_ref[...])


def _mix_out(x, ys, w_out, g):
    t, d = x.shape
    row = pl.BlockSpec((TOKEN_TILE, d), lambda i: (i, 0))
    yspec = pl.BlockSpec((TOKEN_TILE, GROUP_W), lambda i: (i, 0))
    return pl.pallas_call(
        _mix_out_body,
        grid=(t // TOKEN_TILE,),
        in_specs=[row, yspec, yspec, yspec, yspec, _resident((d, d)), _resident((1, d))],
        out_specs=row,
        out_shape=jax.ShapeDtypeStruct((t, d), F32),
        compiler_params=_params("parallel"),
        name="mix_out",
    )(x, *ys, w_out, g.reshape(1, d))


def kernel(x, ffn1_pre_g, ffn1_w_gate, ffn1_w_up, ffn1_w_down, ffn1_post_g, mix_pre_g, w_in, sc_conv_w, sg_ln_w, sg_ln_b, sg_w, sg_b, rk_mu, rk_w0, rk_w_up, rk_a0, rk_a_up, rk_g_up, rk_k_k, rk_k_a, rk_r_k, rk_ln_w, rk_ln_b, cm_conv_w, cm_conv_b, cm_ln_w, cm_ln_b, w_out, mix_post_g, ffn2_pre_g, ffn2_w_gate, ffn2_w_up, ffn2_w_down, ffn2_post_g):
    b, s, d = x.shape
    t = b * s
    bf = lambda w: w.astype(BF16)
    xf = x.reshape(t, d)
    for l in range(ffn1_pre_g.shape[0]):
        xf = _ffn(xf, ffn1_pre_g[l], bf(ffn1_w_gate[l]), bf(ffn1_w_up[l]), bf(ffn1_w_down[l]),
                  ffn1_post_g[l])
        pa, pb, pc, pd = _mix_in(xf, mix_pre_g[l], bf(w_in[l]))
        seq = lambda p: p.reshape(b, s, p.shape[-1])
        ya = _mix_a(seq(pa), sc_conv_w[l])
        yb = _mix_b(seq(pb), sg_ln_w[l], sg_ln_b[l], sg_w[l], sg_b[l])
        yc = _mix_c(seq(pc), rk_mu[l], rk_w0[l], rk_w_up[l], rk_a0[l], rk_a_up[l], rk_g_up[l],
                    rk_k_k[l], rk_k_a[l], rk_r_k[l].reshape(-1), rk_ln_w[l], rk_ln_b[l])
        yd = _mix_d(seq(pd), cm_conv_w[l], cm_conv_b[l], cm_ln_w[l], cm_ln_b[l])
        flat = lambda y: y.reshape(t, GROUP_W)
        xf = _mix_out(xf, (flat(ya), flat(yb), flat(yc), flat(yd)), bf(w_out[l]), mix_post_g[l])
        xf = _ffn(xf, ffn2_pre_g[l], bf(ffn2_w_gate[l]), bf(ffn2_w_up[l]), bf(ffn2_w_down[l]),
                  ffn2_post_g[l])
    return xf.reshape(b, s, d)
```

```python
import functools
import math

import jax
import jax.numpy as jnp
from jax import lax
from jax.experimental import pallas as pl
from jax.experimental.pallas import tpu as pltpu

F32 = jnp.float32
BF16 = jnp.bfloat16

D_MODEL = 1024
D_FF = 2816
GROUP_W = 256
HEAD_DIM = 64
N_HEADS = 4
SC_WIDTH = 3
CHUNK = 128
CM_WIDTH = 31
RANK_W = 32
RANK_A = 32
RANK_G = 64
A_COLS = 3 * GROUP_W
B_COLS = 2 * GROUP_W
C_COLS = 3 * GROUP_W + RANK_W + RANK_A + RANK_G
D_COLS = 2 * GROUP_W
IN_COLS = A_COLS + B_COLS + C_COLS + D_COLS
LORA_W = RANK_W + RANK_A + RANK_G
RMS_EPS = 1e-6
LN_EPS = 1e-5
GN_EPS = 1e-5 * HEAD_DIM
FFN_RESID = 0.5

V7X_VMEM_BYTES = 64 * 1024 * 1024
VMEM_LIMIT_BYTES = V7X_VMEM_BYTES - 8 * 1024 * 1024
SUBLANES = 8

TOKEN_TILE = 512
SEQ_TILE = 512
FFN_CHUNK = 256
RWKV_CHUNK = 64
RWKV_BLOCK = 4
CONV_PAD = 32
CONV_TILE = 128
PAIR_W = 2 * HEAD_DIM


def _params(*sem):
    return pltpu.CompilerParams(dimension_semantics=sem, vmem_limit_bytes=VMEM_LIMIT_BYTES)


def _resident(shape):
    nd = len(shape)
    return pl.BlockSpec(shape, lambda *_: (0,) * nd, pipeline_mode=pl.Buffered(1))


def _mm(a, b):
    return jnp.dot(a.astype(BF16), b.astype(BF16), preferred_element_type=F32)


def _mm_nt(a, b):
    return lax.dot_general(a.astype(BF16), b.astype(BF16), (((1,), (1,)), ((), ())),
                           preferred_element_type=F32)


def _mm_tn(a, b):
    return lax.dot_general(a.astype(BF16), b.astype(BF16), (((0,), (0,)), ((), ())),
                           preferred_element_type=F32)


def _rms(x, g):
    return x * lax.rsqrt(jnp.mean(x * x, axis=-1, keepdims=True) + RMS_EPS) * g


def _layer_norm(x, g, b, eps):
    mu = jnp.mean(x, axis=-1, keepdims=True)
    xc = x - mu
    var = jnp.mean(xc * xc, axis=-1, keepdims=True)
    return xc * lax.rsqrt(var + eps) * g + b


def _ffn_body(x_ref, pre_g_ref, wg_ref, wu_ref, wd_ref, post_g_ref, o_ref, acc_ref):
    x = x_ref[...]
    h = _rms(x, pre_g_ref[...]).astype(BF16)
    for c in range(D_FF // FFN_CHUNK):
        sl = slice(c * FFN_CHUNK, (c + 1) * FFN_CHUNK)
        g = jnp.dot(h, wg_ref[:, sl], preferred_element_type=F32)
        u = jnp.dot(h, wu_ref[:, sl], preferred_element_type=F32)
        act = (g * jax.nn.sigmoid(g) * u).astype(BF16)
        part = jnp.dot(act, wd_ref[sl, :], preferred_element_type=F32)
        if c == 0:
            acc_ref[...] = part
        else:
            acc_ref[...] += part
    o_ref[...] = x + FFN_RESID * _rms(acc_ref[...], post_g_ref[...])


def _ffn(x, pre_g, wg, wu, wd, post_g):
    t, d = x.shape
    f = wg.shape[1]
    row = pl.BlockSpec((TOKEN_TILE, d), lambda i: (i, 0))
    return pl.pallas_call(
        _ffn_body,
        grid=(t // TOKEN_TILE,),
        in_specs=[row, _resident((1, d)), _resident((d, f)), _resident((d, f)),
                  _resident((f, d)), _resident((1, d))],
        out_specs=row,
        out_shape=jax.ShapeDtypeStruct((t, d), F32),
        scratch_shapes=[pltpu.VMEM((TOKEN_TILE, d), F32)],
        compiler_params=_params("parallel"),
        name="ffn",
    )(x, pre_g.reshape(1, d), wg, wu, wd, post_g.reshape(1, d))


def _causal_conv_tile(zpad_ref, w_ref, width, t0):
    zh = zpad_ref[t0:t0 + CONV_TILE + CONV_PAD, :]
    acc = None
    for r in range(min(SUBLANES, width)):
        zr = zh if r == 0 else pltpu.roll(zh, r, 0)
        for q in range((width - 1 - r) // SUBLANES + 1):
            j = width - 1 - (SUBLANES * q + r)
            start = CONV_PAD - SUBLANES * q
            term = zr[start:start + CONV_TILE] * w_ref[j:j + 1, :]
            acc = term if acc is None else acc + term
    return acc


_OFF_B = A_COLS
_OFF_C = A_COLS + B_COLS
_OFF_D = A_COLS + B_COLS + C_COLS


def _mix_in_body(x_ref, g_ref, w_ref, scw_ref, sg_lnw_ref, sg_lnb_ref, sgw_ref, sgb_ref,
                 cmw_ref, cmb_ref, cm_lnw_ref, cm_lnb_ref,
                 ya_ref, yb_ref, pc_ref, yd_ref, za_ref, zd_ref):
    g = GROUP_W
    ts = x_ref.shape[0]
    first = pl.program_id(1) == 0

    @pl.when(first)
    def _():
        za_ref[0:CONV_PAD, :] = jnp.zeros((CONV_PAD, g), F32)
        zd_ref[0:CONV_PAD, :] = jnp.zeros((CONV_PAD, g), F32)

    @pl.when(jnp.logical_not(first))
    def _():
        za_ref[0:CONV_PAD, :] = za_ref[ts:ts + CONV_PAD, :]
        zd_ref[0:CONV_PAD, :] = zd_ref[ts:ts + CONV_PAD, :]

    h = _rms(x_ref[...], g_ref[...]).astype(BF16)
    pd = jnp.dot(h, w_ref[:, _OFF_D:_OFF_D + D_COLS], preferred_element_type=F32)
    zd_ref[CONV_PAD:, :] = pd[:, 0:g] * jax.nn.sigmoid(pd[:, g:2 * g])
    pa = jnp.dot(h, w_ref[:, 0:A_COLS], preferred_element_type=F32)
    za_ref[CONV_PAD:, :] = pa[:, g:2 * g] * pa[:, 2 * g:3 * g]
    pc_ref[...] = jnp.dot(h, w_ref[:, _OFF_C:_OFF_C + C_COLS], preferred_element_type=F32)
    pb = jnp.dot(h, w_ref[:, _OFF_B:_OFF_B + B_COLS], preferred_element_type=F32)

    for t0 in range(0, ts, CONV_TILE):
        rows = slice(t0, t0 + CONV_TILE)
        y = _causal_conv_tile(zd_ref, cmw_ref, CM_WIDTH, t0) + cmb_ref[...]
        y = _layer_norm(y, cm_lnw_ref[...], cm_lnb_ref[...], LN_EPS)
        yd_ref[rows, :] = (y * jax.nn.sigmoid(y)).astype(BF16)
        ya_ref[rows, :] = (pa[rows, 0:g] * _causal_conv_tile(za_ref, scw_ref, SC_WIDTH, t0)).astype(BF16)

    row = lax.broadcasted_iota(jnp.int32, (CHUNK, CHUNK), 0)
    col = lax.broadcasted_iota(jnp.int32, (CHUNK, CHUNK), 1)
    w_tril = [jnp.where(row >= col, sgw_ref[hd], 0.0).astype(BF16) for hd in range(N_HEADS)]
    lane_head = lax.broadcasted_iota(jnp.int32, (CHUNK, g), 1) // HEAD_DIM
    for t0 in range(0, ts, CHUNK):
        rows = slice(t0, t0 + CHUNK)
        v = _layer_norm(pb[rows, g:2 * g], sg_lnw_ref[...], sg_lnb_ref[...], LN_EPS).astype(BF16)
        s = sgb_ref[...]
        for hd in range(N_HEADS):
            sh = jnp.dot(w_tril[hd], v, preferred_element_type=F32)
            s = s + jnp.where(lane_head == hd, sh, 0.0)
        yb_ref[rows, :] = (pb[rows, 0:g] * s).astype(BF16)


def _mix_in(x, seq, g, w_in, sc_conv_w, sg_ln_w, sg_ln_b, sg_w, sg_b, cm_conv_w, cm_conv_b, cm_ln_w, cm_ln_b):
    t, d = x.shape
    gw = GROUP_W
    steps = seq // SEQ_TILE
    vec = lambda v: v.reshape(1, gw)
    sg_bias = jnp.repeat(sg_b.T, HEAD_DIM, axis=1)
    tile = lambda width: pl.BlockSpec((SEQ_TILE, width), lambda b, j: (b * steps + j, 0))
    return pl.pallas_call(
        _mix_in_body,
        grid=(t // seq, steps),
        in_specs=[tile(d), _resident((1, d)), _resident((d, IN_COLS)), _resident((SC_WIDTH, gw)),
                  _resident((1, gw)), _resident((1, gw)), _resident((N_HEADS, CHUNK, CHUNK)),
                  _resident((CHUNK, gw)), _resident((CM_WIDTH, gw)), _resident((1, gw)),
                  _resident((1, gw)), _resident((1, gw))],
        out_specs=[tile(gw), tile(gw), tile(C_COLS), tile(gw)],
        out_shape=[jax.ShapeDtypeStruct((t, gw), BF16), jax.ShapeDtypeStruct((t, gw), BF16),
                   jax.ShapeDtypeStruct((t, C_COLS), F32), jax.ShapeDtypeStruct((t, gw), BF16)],
        scratch_shapes=[pltpu.VMEM((SEQ_TILE + CONV_PAD, gw), F32), pltpu.VMEM((SEQ_TILE + CONV_PAD, gw), F32)],
        compiler_params=_params("parallel", "arbitrary"),
        name="mix_in",
    )(x, g.reshape(1, d), w_in, sc_conv_w, vec(sg_ln_w), vec(sg_ln_b), sg_w, sg_bias,
      cm_conv_w, vec(cm_conv_b), vec(cm_ln_w), vec(cm_ln_b))


def _split2(x):
    hi = x.astype(BF16)
    lo = (x - hi.astype(F32)).astype(BF16)
    return hi, lo


def _head_sum(x, ones_bd):
    hi, lo = _split2(x)
    return (jnp.dot(hi, ones_bd, preferred_element_type=F32)
            + jnp.dot(lo, ones_bd, preferred_element_type=F32))


def _rwkv_body(pc_ref, mu_ref, w0_ref, lora_ref, a0_ref, kk_ref, ka_ref, rk_ref, lnw_ref, lnb_ref,
               o_ref, gm_ref, yc_ref, rec_ref, bonus_ref, gate_ref, state_ref, prev_ref):
    g = GROUP_W
    lc = RWKV_CHUNK
    tb = RWKV_BLOCK * lc
    seq = pc_ref.shape[1]

    row = lax.broadcasted_iota(jnp.int32, (lc, lc), 0)
    col = lax.broadcasted_iota(jnp.int32, (lc, lc), 1)
    strict = row > col
    incl = row >= col
    brow = lax.broadcasted_iota(jnp.int32, (tb, tb), 0)
    bcol = lax.broadcasted_iota(jnp.int32, (tb, tb), 1)
    tri_ones = jnp.where((brow // lc == bcol // lc) & (brow >= bcol), 1.0, 0.0).astype(BF16)
    bd_r = lax.broadcasted_iota(jnp.int32, (g, g), 0) // HEAD_DIM
    bd_c = lax.broadcasted_iota(jnp.int32, (g, g), 1) // HEAD_DIM
    ones_bd = jnp.where(bd_r == bd_c, 1.0, 0.0).astype(BF16)
    pr = lax.broadcasted_iota(jnp.int32, (PAIR_W, PAIR_W), 0)
    pcol = lax.broadcasted_iota(jnp.int32, (PAIR_W, PAIR_W), 1)
    pair_bd = (pr // HEAD_DIM) == (pcol // HEAD_DIM)
    pair_eye = pr == pcol
    first_head = lax.broadcasted_iota(jnp.int32, (lc, PAIR_W), 1) < HEAD_DIM
    first_head2 = jnp.concatenate([first_head, first_head], axis=1)
    row0 = lax.broadcasted_iota(jnp.int32, (tb, C_COLS), 0) == 0

    state_ref[...] = jnp.zeros(state_ref.shape, F32)
    prev_ref[...] = jnp.zeros(prev_ref.shape, F32)

    def prepare(i, carry):
        t0 = pl.multiple_of(i * tb, tb)
        x = pc_ref[0, pl.ds(t0, tb), :]
        xs = jnp.where(row0, prev_ref[...], pltpu.roll(x, 1, 0))
        prev_ref[...] = x[tb - 1:tb, :]
        xm = x + (xs - x) * mu_ref[...]
        r = xm[:, 0:g]
        k = xm[:, g:2 * g]
        v = xm[:, 2 * g:3 * g]
        lo_in = xm[:, 3 * g:3 * g + LORA_W]
        lane = lax.broadcasted_iota(jnp.int32, (tb, LORA_W), 1)
        lo_act = jnp.where(lane < RANK_W, jnp.tanh(lo_in),
                           jnp.where(lane < RANK_W + RANK_A, lo_in, jax.nn.sigmoid(lo_in)))
        lora = _mm(lo_act, lora_ref[...])
        e = jax.nn.sigmoid(w0_ref[...] + lora[:, 0:g]) * math.exp(-0.5)
        a_sig = jax.nn.sigmoid(a0_ref[...] + lora[:, g:2 * g])
        gate_ref[pl.ds(t0, tb), :] = lora[:, 2 * g:3 * g]
        kk = k * kk_ref[...]
        kk = kk / jnp.maximum(jnp.sqrt(_head_sum(kk * kk, ones_bd)), 1e-12)
        k = k * (1.0 + (a_sig - 1.0) * ka_ref[...])
        a_ = -kk
        b_ = kk * a_sig
        bonus_ref[pl.ds(t0, tb), :] = _head_sum(r * k * rk_ref[...], ones_bd) * v

        e_hi = e.astype(BF16)
        e_r = e - e_hi.astype(F32)
        e_mid = e_r.astype(BF16)
        e_lo = (e_r - e_mid.astype(F32)).astype(BF16)
        cs = (jnp.dot(tri_ones, e_hi, preferred_element_type=F32)
              + jnp.dot(tri_ones, e_mid, preferred_element_type=F32)
              + jnp.dot(tri_ones, e_lo, preferred_element_type=F32))
        cs_last = jnp.concatenate(
            [jnp.broadcast_to(cs[(ci + 1) * lc - 1:(ci + 1) * lc, :], (lc, g)) for ci in range(RWKV_BLOCK)],
            axis=0)
        w_inc = jnp.exp(-cs)
        w_exc = jnp.exp(e - cs)
        w_inv = jnp.exp(cs)
        w_fin = jnp.exp(cs - cs_last)
        w_chunk_all = jnp.exp(-cs_last)
        at_all = a_ * w_exc
        rt_all = r * w_inc
        bt_all = b_ * w_inv
        kt_all = k * w_inv
        bh_all = b_ * w_fin
        kh_all = k * w_fin

        probs = [(ci, p) for ci in range(RWKV_BLOCK) for p in range(N_HEADS // 2)]
        heads = (first_head, ~first_head)
        ops = []
        for ci, p in probs:
            rs = slice(ci * lc, (ci + 1) * lc)
            ps = slice(p * PAIR_W, (p + 1) * PAIR_W)
            ops.append(tuple(t[rs, ps] for t in (at_all, rt_all, bt_all, kt_all, bh_all, kh_all, v)))
        a_ab, a_ak, a_rb, a_rk = [], [], [], []
        for at, rt, bt, kt, bh, kh, vp in ops:
            for hm in heads:
                lhs = jnp.concatenate([jnp.where(hm, at, 0.0), jnp.where(hm, rt, 0.0)], axis=0)
                ab = _mm_nt(lhs, bt)
                ak = _mm_nt(lhs, kt)
                a_ab.append(jnp.where(strict, ab[0:lc], 0.0))
                a_ak.append(jnp.where(strict, ak[0:lc], 0.0))
                a_rb.append(jnp.where(incl, ab[lc:], 0.0))
                a_rk.append(jnp.where(incl, ak[lc:], 0.0))
        zs = []
        for n, (at, rt, bt, kt, bh, kh, vp) in enumerate(ops):
            q0 = [_mm(a_ak[2 * n + h], vp) for h in range(2)]
            z0 = jnp.concatenate([at, jnp.where(first_head, q0[0], q0[1])], axis=1)
            zs += [z0, z0]
        aps = list(a_ab)
        n_fac = int(math.log2(lc))
        for f in range(n_fac):
            zs = [z + _mm(ap, z) for ap, z in zip(aps, zs)]
            if f + 1 < n_fac:
                aps = [_mm(ap, ap) for ap in aps]
        for n, ((ci, p), (at, rt, bt, kt, bh, kh, vp)) in enumerate(zip(probs, ops)):
            ps = slice(p * PAIR_W, (p + 1) * PAIR_W)
            idx = i * RWKV_BLOCK + ci
            z = jnp.where(first_head2, zs[2 * n], zs[2 * n + 1])
            gy = [_mm(a_rb[2 * n + h], z) for h in range(2)]
            yk = [_mm(a_rk[2 * n + h], vp) for h in range(2)]
            g_mat = rt + jnp.where(first_head, gy[0][:, 0:PAIR_W], gy[1][:, 0:PAIR_W])
            y0 = jnp.where(first_head, gy[0][:, PAIR_W:] + yk[0], gy[1][:, PAIR_W:] + yk[1])
            mc = _mm_tn(bh, z)
            kv = _mm_tn(kh, vp)
            w_chunk = w_chunk_all[ci * lc:ci * lc + 1, ps]
            m_mat = jnp.where(pair_bd, mc[:, 0:PAIR_W], 0.0) + jnp.where(pair_eye, w_chunk, 0.0)
            c_mat = jnp.where(pair_bd, mc[:, PAIR_W:] + kv, 0.0)
            gm_ref[idx, p, 0:lc, :] = g_mat.astype(BF16)
            gm_ref[idx, p, lc:, :] = m_mat.astype(BF16)
            yc_ref[idx, p, 0:lc, :] = y0
            yc_ref[idx, p, lc:, :] = c_mat
        return carry

    def scan(c, carry):
        t0 = pl.multiple_of(c * lc, lc)
        for p in range(N_HEADS // 2):
            step = jnp.dot(gm_ref[c, p], state_ref[p].astype(BF16), preferred_element_type=F32)
            step = step + yc_ref[c, p]
            rec_ref[pl.ds(t0, lc), p * PAIR_W:(p + 1) * PAIR_W] = step[0:lc]
            state_ref[p] = step[lc:]
        return carry

    def finish(i, carry):
        t0 = pl.multiple_of(i * tb, tb)
        o = rec_ref[pl.ds(t0, tb), :]
        mean = _head_sum(o, ones_bd) * (1.0 / HEAD_DIM)
        oc = o - mean
        var = _head_sum(oc * oc, ones_bd) * (1.0 / HEAD_DIM)
        o = oc * lax.rsqrt(var + GN_EPS) * lnw_ref[...] + lnb_ref[...]
        o = (o + bonus_ref[pl.ds(t0, tb), :]) * gate_ref[pl.ds(t0, tb), :]
        o_ref[0, pl.ds(t0, tb), :] = o.astype(BF16)
        return carry

    lax.fori_loop(0, seq // tb, prepare, 0)
    lax.fori_loop(0, seq // lc, scan, 0)
    lax.fori_loop(0, seq // tb, finish, 0)


def _mix_c(pc, mu, w0, w_up, a0, a_up, g_up, k_k, k_a, r_k, ln_w, ln_b):
    b, s, _ = pc.shape
    g = GROUP_W
    n_pairs = N_HEADS // 2
    lora = jnp.zeros((LORA_W, 3 * g), F32)
    lora = lora.at[0:RANK_W, 0:g].set(w_up)
    lora = lora.at[RANK_W:RANK_W + RANK_A, g:2 * g].set(a_up)
    lora = lora.at[RANK_W + RANK_A:, 2 * g:].set(g_up)
    vec = lambda t: t.reshape(1, g)
    return pl.pallas_call(
        _rwkv_body,
        grid=(b,),
        in_specs=[pl.BlockSpec((1, s, C_COLS), lambda i: (i, 0, 0)), _resident((1, C_COLS)),
                  _resident((1, g)), _resident((LORA_W, 3 * g)), _resident((1, g)), _resident((1, g)),
                  _resident((1, g)), _resident((1, g)), _resident((1, g)), _resident((1, g))],
        out_specs=pl.BlockSpec((1, s, g), lambda i: (i, 0, 0)),
        out_shape=jax.ShapeDtypeStruct((b, s, g), BF16),
        scratch_shapes=[
            pltpu.VMEM((s // RWKV_CHUNK, n_pairs, RWKV_CHUNK + PAIR_W, PAIR_W), BF16),
            pltpu.VMEM((s // RWKV_CHUNK, n_pairs, RWKV_CHUNK + PAIR_W, PAIR_W), F32),
            pltpu.VMEM((s, g), F32),
            pltpu.VMEM((s, g), F32),
            pltpu.VMEM((s, g), F32),
            pltpu.VMEM((n_pairs, PAIR_W, PAIR_W), F32),
            pltpu.VMEM((1, C_COLS), F32),
        ],
        compiler_params=_params("parallel"),
        name="mix_c",
    )(pc, mu.reshape(1, C_COLS), vec(w0), lora.astype(BF16), vec(a0), vec(k_k), vec(k_a), vec(r_k),
      vec(ln_w), vec(ln_b))


def _mix_out_body(x_ref, ya_ref, yb_ref, yc_ref, yd_ref, w_ref, g_ref, o_ref):
    m = None
    for i, y_ref in enumerate((ya_ref, yb_ref, yc_ref, yd_ref)):
        part = jnp.dot(y_ref[...], w_ref[i * GROUP_W:(i + 1) * GROUP_W, :], preferred_element_type=F32)
        m = part if m is None else m + part
    o_ref[...] = x_ref[...] + _rms(m, g_ref[...])


def _mix_out(x, ys, w_out, g):
    t, d = x.shape
    row = pl.BlockSpec((TOKEN_TILE, d), lambda i: (i, 0))
    yspec = pl.BlockSpec((TOKEN_TILE, GROUP_W), lambda i: (i, 0))
    return pl.pallas_call(
        _mix_out_body,
        grid=(t // TOKEN_TILE,),
        in_specs=[row, yspec, yspec, yspec, yspec, _resident((d, d)), _resident((1, d))],
        out_specs=row,
        out_shape=jax.ShapeDtypeStruct((t, d), F32),
        compiler_params=_params("parallel"),
        name="mix_out",
    )(x, *ys, w_out, g.reshape(1, d))


def kernel(x, ffn1_pre_g, ffn1_w_gate, ffn1_w_up, ffn1_w_down, ffn1_post_g, mix_pre_g, w_in, sc_conv_w, sg_ln_w, sg_ln_b, sg_w, sg_b, rk_mu, rk_w0, rk_w_up, rk_a0, rk_a_up, rk_g_up, rk_k_k, rk_k_a, rk_r_k, rk_ln_w, rk_ln_b, cm_conv_w, cm_conv_b, cm_ln_w, cm_ln_b, w_out, mix_post_g, ffn2_pre_g, ffn2_w_gate, ffn2_w_up, ffn2_w_down, ffn2_post_g):
    b, s, d = x.shape
    t = b * s
    bf = lambda w: w.astype(BF16)
    xf = x.reshape(t, d)
    for l in range(ffn1_pre_g.shape[0]):
        xf = _ffn(xf, ffn1_pre_g[l], bf(ffn1_w_gate[l]), bf(ffn1_w_up[l]), bf(ffn1_w_down[l]),
                  ffn1_post_g[l])
        ya, yb, pc, yd = _mix_in(xf, s, mix_pre_g[l], bf(w_in[l]), sc_conv_w[l], sg_ln_w[l], sg_ln_b[l],
                                 sg_w[l], sg_b[l], cm_conv_w[l], cm_conv_b[l], cm_ln_w[l], cm_ln_b[l])
        yc = _mix_c(pc.reshape(b, s, C_COLS), rk_mu[l], rk_w0[l], rk_w_up[l], rk_a0[l], rk_a_up[l],
                    rk_g_up[l], rk_k_k[l], rk_k_a[l], rk_r_k[l].reshape(-1), rk_ln_w[l], rk_ln_b[l])
        xf = _mix_out(xf, (ya, yb, yc.reshape(t, GROUP_W), yd), bf(w_out[l]), mix_post_g[l])
        xf = _ffn(xf, ffn2_pre_g[l], bf(ffn2_w_gate[l]), bf(ffn2_w_up[l]), bf(ffn2_w_down[l]),
                  ffn2_post_g[l])
    return xf.reshape(b, s, d)
```

```python
import functools
import math

import jax
import jax.numpy as jnp
from jax import lax
from jax.experimental import pallas as pl
from jax.experimental.pallas import tpu as pltpu

F32 = jnp.float32
BF16 = jnp.bfloat16

D_MODEL = 1024
D_FF = 2816
GROUP_W = 256
HEAD_DIM = 64
N_HEADS = 4
SC_WIDTH = 3
CHUNK = 128
CM_WIDTH = 31
RANK_W = 32
RANK_A = 32
RANK_G = 64
A_COLS = 3 * GROUP_W
B_COLS = 2 * GROUP_W
C_COLS = 3 * GROUP_W + RANK_W + RANK_A + RANK_G
D_COLS = 2 * GROUP_W
IN_COLS = A_COLS + B_COLS + C_COLS + D_COLS
LORA_W = RANK_W + RANK_A + RANK_G
RMS_EPS = 1e-6
LN_EPS = 1e-5
GN_EPS = 1e-5 * HEAD_DIM
FFN_RESID = 0.5

V7X_VMEM_BYTES = 64 * 1024 * 1024
VMEM_LIMIT_BYTES = V7X_VMEM_BYTES - 8 * 1024 * 1024
SUBLANES = 8

FFN_TILE = 512
SEQ_TILE = 512
FFN_CHUNK = 256
RWKV_CHUNK = 64
RWKV_BLOCK = 4
CONV_PAD = 32
CONV_TILE = 128
PAIR_W = 2 * HEAD_DIM


def _params(*sem):
    return pltpu.CompilerParams(dimension_semantics=sem, vmem_limit_bytes=VMEM_LIMIT_BYTES)


def _resident(shape):
    nd = len(shape)
    return pl.BlockSpec(shape, lambda *_: (0,) * nd, pipeline_mode=pl.Buffered(1))


def _mm(a, b):
    return jnp.dot(a.astype(BF16), b.astype(BF16), preferred_element_type=F32)


def _mm_nt(a, b):
    return lax.dot_general(a.astype(BF16), b.astype(BF16), (((1,), (1,)), ((), ())),
                           preferred_element_type=F32)


def _mm_tn(a, b):
    return lax.dot_general(a.astype(BF16), b.astype(BF16), (((0,), (0,)), ((), ())),
                           preferred_element_type=F32)


def _rms(x, g):
    return x * lax.rsqrt(jnp.mean(x * x, axis=-1, keepdims=True) + RMS_EPS) * g


def _layer_norm(x, g, b, eps):
    mu = jnp.mean(x, axis=-1, keepdims=True)
    xc = x - mu
    var = jnp.mean(xc * xc, axis=-1, keepdims=True)
    return xc * lax.rsqrt(var + eps) * g + b


def _ffn_tail(x, pre_g_ref, wg_ref, wu_ref, wd_ref, post_g_ref, o_ref, acc_ref):
    h = _rms(x, pre_g_ref[...]).astype(BF16)
    for c in range(D_FF // FFN_CHUNK):
        sl = slice(c * FFN_CHUNK, (c + 1) * FFN_CHUNK)
        g = jnp.dot(h, wg_ref[:, sl], preferred_element_type=F32)
        u = jnp.dot(h, wu_ref[:, sl], preferred_element_type=F32)
        act = (g * jax.nn.sigmoid(g) * u).astype(BF16)
        part = jnp.dot(act, wd_ref[sl, :], preferred_element_type=F32)
        if c == 0:
            acc_ref[...] = part
        else:
            acc_ref[...] += part
    o_ref[...] = x + FFN_RESID * _rms(acc_ref[...], post_g_ref[...])


def _ffn_body(x_ref, *ffn_refs):
    _ffn_tail(x_ref[...], *ffn_refs)


def _mix_out_ffn_body(x_ref, ya_ref, yb_ref, yc_ref, yd_ref, wo_ref, mix_g_ref, *ffn_refs):
    m = None
    for i, y_ref in enumerate((ya_ref, yb_ref, yc_ref, yd_ref)):
        part = jnp.dot(y_ref[...], wo_ref[i * GROUP_W:(i + 1) * GROUP_W, :], preferred_element_type=F32)
        m = part if m is None else m + part
    _ffn_tail(x_ref[...] + _rms(m, mix_g_ref[...]), *ffn_refs)


def _ffn(x, pre_g, wg, wu, wd, post_g, mix=None):
    t, d = x.shape
    f = wg.shape[1]
    row = pl.BlockSpec((FFN_TILE, d), lambda i: (i, 0))
    ffn_specs = [_resident((1, d)), _resident((d, f)), _resident((d, f)), _resident((f, d)), _resident((1, d))]
    ffn_args = (pre_g.reshape(1, d), wg, wu, wd, post_g.reshape(1, d))
    if mix is None:
        body, specs, args = _ffn_body, [row], (x,)
    else:
        ys, w_out, mix_g = mix
        yspec = pl.BlockSpec((FFN_TILE, GROUP_W), lambda i: (i, 0))
        body = _mix_out_ffn_body
        specs = [row] + [yspec] * len(ys) + [_resident((d, d)), _resident((1, d))]
        args = (x, *ys, w_out, mix_g.reshape(1, d))
    return pl.pallas_call(
        body,
        grid=(t // FFN_TILE,),
        in_specs=specs + ffn_specs,
        out_specs=row,
        out_shape=jax.ShapeDtypeStruct((t, d), F32),
        scratch_shapes=[pltpu.VMEM((FFN_TILE, d), F32)],
        compiler_params=_params("parallel"),
        name="ffn" if mix is None else "mix_out_ffn",
    )(*args, *ffn_args)


def _causal_conv_tile(zpad_ref, w_ref, width, t0):
    zh = zpad_ref[t0:t0 + CONV_TILE + CONV_PAD, :]
    acc = None
    for r in range(min(SUBLANES, width)):
        zr = zh if r == 0 else pltpu.roll(zh, r, 0)
        for q in range((width - 1 - r) // SUBLANES + 1):
            j = width - 1 - (SUBLANES * q + r)
            start = CONV_PAD - SUBLANES * q
            term = zr[start:start + CONV_TILE] * w_ref[j:j + 1, :]
            acc = term if acc is None else acc + term
    return acc


_OFF_B = A_COLS
_OFF_C = A_COLS + B_COLS
_OFF_D = A_COLS + B_COLS + C_COLS


def _mix_in_body(x_ref, g_ref, w_ref, scw_ref, sg_lnw_ref, sg_lnb_ref, sgw_ref, sgb_ref,
                 cmw_ref, cmb_ref, cm_lnw_ref, cm_lnb_ref,
                 ya_ref, yb_ref, pc_ref, yd_ref, za_ref, zd_ref):
    g = GROUP_W
    ts = x_ref.shape[0]
    first = pl.program_id(1) == 0

    @pl.when(first)
    def _():
        za_ref[0:CONV_PAD, :] = jnp.zeros((CONV_PAD, g), F32)
        zd_ref[0:CONV_PAD, :] = jnp.zeros((CONV_PAD, g), F32)

    @pl.when(jnp.logical_not(first))
    def _():
        za_ref[0:CONV_PAD, :] = za_ref[ts:ts + CONV_PAD, :]
        zd_ref[0:CONV_PAD, :] = zd_ref[ts:ts + CONV_PAD, :]

    h = _rms(x_ref[...], g_ref[...]).astype(BF16)
    pd = jnp.dot(h, w_ref[:, _OFF_D:_OFF_D + D_COLS], preferred_element_type=F32)
    zd_ref[CONV_PAD:, :] = pd[:, 0:g] * jax.nn.sigmoid(pd[:, g:2 * g])
    pa = jnp.dot(h, w_ref[:, 0:A_COLS], preferred_element_type=F32)
    za_ref[CONV_PAD:, :] = pa[:, g:2 * g] * pa[:, 2 * g:3 * g]
    pc_ref[...] = jnp.dot(h, w_ref[:, _OFF_C:_OFF_C + C_COLS], preferred_element_type=F32)
    pb = jnp.dot(h, w_ref[:, _OFF_B:_OFF_B + B_COLS], preferred_element_type=F32)

    for t0 in range(0, ts, CONV_TILE):
        rows = slice(t0, t0 + CONV_TILE)
        y = _causal_conv_tile(zd_ref, cmw_ref, CM_WIDTH, t0) + cmb_ref[...]
        y = _layer_norm(y, cm_lnw_ref[...], cm_lnb_ref[...], LN_EPS)
        yd_ref[rows, :] = (y * jax.nn.sigmoid(y)).astype(BF16)
        ya_ref[rows, :] = (pa[rows, 0:g] * _causal_conv_tile(za_ref, scw_ref, SC_WIDTH, t0)).astype(BF16)

    row = lax.broadcasted_iota(jnp.int32, (CHUNK, CHUNK), 0)
    col = lax.broadcasted_iota(jnp.int32, (CHUNK, CHUNK), 1)
    w_tril = [jnp.where(row >= col, sgw_ref[hd], 0.0).astype(BF16) for hd in range(N_HEADS)]
    lane_head = lax.broadcasted_iota(jnp.int32, (CHUNK, g), 1) // HEAD_DIM
    for t0 in range(0, ts, CHUNK):
        rows = slice(t0, t0 + CHUNK)
        v = _layer_norm(pb[rows, g:2 * g], sg_lnw_ref[...], sg_lnb_ref[...], LN_EPS).astype(BF16)
        s = sgb_ref[...]
        for hd in range(N_HEADS):
            sh = jnp.dot(w_tril[hd], v, preferred_element_type=F32)
            s = s + jnp.where(lane_head == hd, sh, 0.0)
        yb_ref[rows, :] = (pb[rows, 0:g] * s).astype(BF16)


def _mix_in(x, seq, g, w_in, sc_conv_w, sg_ln_w, sg_ln_b, sg_w, sg_b, cm_conv_w, cm_conv_b, cm_ln_w, cm_ln_b):
    t, d = x.shape
    gw = GROUP_W
    steps = seq // SEQ_TILE
    vec = lambda v: v.reshape(1, gw)
    sg_bias = jnp.repeat(sg_b.T, HEAD_DIM, axis=1)
    tile = lambda width: pl.BlockSpec((SEQ_TILE, width), lambda b, j: (b * steps + j, 0))
    return pl.pallas_call(
        _mix_in_body,
        grid=(t // seq, steps),
        in_specs=[tile(d), _resident((1, d)), _resident((d, IN_COLS)), _resident((SC_WIDTH, gw)),
                  _resident((1, gw)), _resident((1, gw)), _resident((N_HEADS, CHUNK, CHUNK)),
                  _resident((CHUNK, gw)), _resident((CM_WIDTH, gw)), _resident((1, gw)),
                  _resident((1, gw)), _resident((1, gw))],
        out_specs=[tile(gw), tile(gw), tile(C_COLS), tile(gw)],
        out_shape=[jax.ShapeDtypeStruct((t, gw), BF16), jax.ShapeDtypeStruct((t, gw), BF16),
                   jax.ShapeDtypeStruct((t, C_COLS), F32), jax.ShapeDtypeStruct((t, gw), BF16)],
        scratch_shapes=[pltpu.VMEM((SEQ_TILE + CONV_PAD, gw), F32), pltpu.VMEM((SEQ_TILE + CONV_PAD, gw), F32)],
        compiler_params=_params("parallel", "arbitrary"),
        name="mix_in",
    )(x, g.reshape(1, d), w_in, sc_conv_w, vec(sg_ln_w), vec(sg_ln_b), sg_w, sg_bias,
      cm_conv_w, vec(cm_conv_b), vec(cm_ln_w), vec(cm_ln_b))


def _split2(x):
    hi = x.astype(BF16)
    lo = (x - hi.astype(F32)).astype(BF16)
    return hi, lo


def _head_sum(x, ones_bd):
    hi, lo = _split2(x)
    return (jnp.dot(hi, ones_bd, preferred_element_type=F32)
            + jnp.dot(lo, ones_bd, preferred_element_type=F32))


def _rwkv_body(pc_ref, mu_ref, w0_ref, lora_ref, a0_ref, kk_ref, ka_ref, rk_ref, lnw_ref, lnb_ref,
               o_ref, gm_ref, yc_ref, rec_ref, bonus_ref, gate_ref, state_ref, prev_ref):
    g = GROUP_W
    lc = RWKV_CHUNK
    tb = RWKV_BLOCK * lc
    seq = pc_ref.shape[1]

    row = lax.broadcasted_iota(jnp.int32, (lc, 2 * lc), 0)
    col = lax.broadcasted_iota(jnp.int32, (lc, 2 * lc), 1) % lc
    strict2 = row > col
    incl2 = row >= col
    brow = lax.broadcasted_iota(jnp.int32, (tb, tb), 0)
    bcol = lax.broadcasted_iota(jnp.int32, (tb, tb), 1)
    tri_ones = jnp.where((brow // lc == bcol // lc) & (brow >= bcol), 1.0, 0.0).astype(BF16)
    bd_r = lax.broadcasted_iota(jnp.int32, (g, g), 0) // HEAD_DIM
    bd_c = lax.broadcasted_iota(jnp.int32, (g, g), 1) // HEAD_DIM
    ones_bd = jnp.where(bd_r == bd_c, 1.0, 0.0).astype(BF16)
    pr = lax.broadcasted_iota(jnp.int32, (PAIR_W, PAIR_W), 0)
    pcol = lax.broadcasted_iota(jnp.int32, (PAIR_W, PAIR_W), 1)
    pair_bd = (pr // HEAD_DIM) == (pcol // HEAD_DIM)
    pair_eye = pr == pcol
    first_head = lax.broadcasted_iota(jnp.int32, (lc, PAIR_W), 1) < HEAD_DIM
    first_head2 = jnp.concatenate([first_head, first_head], axis=1)
    row0 = lax.broadcasted_iota(jnp.int32, (tb, C_COLS), 0) == 0

    def split(y):
        yb = y.astype(BF16)
        m = first_head if y.shape[1] == PAIR_W else first_head2
        zero = jnp.zeros_like(yb)
        return jnp.concatenate([jnp.where(m, yb, zero), jnp.where(m, zero, yb)], axis=0)

    state_ref[...] = jnp.zeros(state_ref.shape, F32)
    prev_ref[...] = jnp.zeros(prev_ref.shape, F32)

    def build(i):
        t0 = pl.multiple_of(i * tb, tb)
        x = pc_ref[0, pl.ds(t0, tb), :]
        xs = jnp.where(row0, prev_ref[...], pltpu.roll(x, 1, 0))
        prev_ref[...] = x[tb - 1:tb, :]
        xm = x + (xs - x) * mu_ref[...]
        r = xm[:, 0:g]
        k = xm[:, g:2 * g]
        v = xm[:, 2 * g:3 * g]
        lo_in = xm[:, 3 * g:3 * g + LORA_W]
        lane = lax.broadcasted_iota(jnp.int32, (tb, LORA_W), 1)
        lo_act = jnp.where(lane < RANK_W, jnp.tanh(lo_in),
                           jnp.where(lane < RANK_W + RANK_A, lo_in, jax.nn.sigmoid(lo_in)))
        lora = _mm(lo_act, lora_ref[...])
        e = jax.nn.sigmoid(w0_ref[...] + lora[:, 0:g]) * math.exp(-0.5)
        a_sig = jax.nn.sigmoid(a0_ref[...] + lora[:, g:2 * g])
        gate_ref[pl.ds(t0, tb), :] = lora[:, 2 * g:3 * g]
        kk = k * kk_ref[...]
        kk = kk / jnp.maximum(jnp.sqrt(_head_sum(kk * kk, ones_bd)), 1e-12)
        k = k * (1.0 + (a_sig - 1.0) * ka_ref[...])
        a_ = -kk
        b_ = kk * a_sig
        bonus_ref[pl.ds(t0, tb), :] = _head_sum(r * k * rk_ref[...], ones_bd) * v

        e_hi = e.astype(BF16)
        e_r = e - e_hi.astype(F32)
        e_mid = e_r.astype(BF16)
        e_lo = (e_r - e_mid.astype(F32)).astype(BF16)
        cs = (jnp.dot(tri_ones, e_hi, preferred_element_type=F32)
              + jnp.dot(tri_ones, e_mid, preferred_element_type=F32)
              + jnp.dot(tri_ones, e_lo, preferred_element_type=F32))
        cs_last = jnp.concatenate(
            [jnp.broadcast_to(cs[(ci + 1) * lc - 1:(ci + 1) * lc, :], (lc, g)) for ci in range(RWKV_BLOCK)],
            axis=0)
        w_inc = jnp.exp(-cs)
        w_exc = jnp.exp(e - cs)
        w_inv = jnp.exp(cs)
        w_fin = jnp.exp(cs - cs_last)
        w_chunk_all = jnp.exp(-cs_last)
        at_all = a_ * w_exc
        rt_all = r * w_inc
        bt_all = b_ * w_inv
        kt_all = k * w_inv
        bh_all = b_ * w_fin
        kh_all = k * w_fin
        yield

        probs = [(ci, p) for ci in range(RWKV_BLOCK) for p in range(N_HEADS // 2)]
        ops = []
        for ci, p in probs:
            rs = slice(ci * lc, (ci + 1) * lc)
            ps = slice(p * PAIR_W, (p + 1) * PAIR_W)
            ops.append(tuple(t[rs, ps] for t in (at_all, rt_all, bt_all, kt_all, bh_all, kh_all, v)))
        a_ab, a_ak, a_rb, a_rk = [], [], [], []
        for at, rt, bt, kt, bh, kh, vp in ops:
            lhs = jnp.concatenate([at, rt], axis=0)
            ab = _mm_nt(lhs, split(bt))
            ak = _mm_nt(lhs, split(kt))
            a_ab.append(jnp.where(strict2, ab[0:lc], 0.0))
            a_ak.append(jnp.where(strict2, ak[0:lc], 0.0))
            a_rb.append(jnp.where(incl2, ab[lc:], 0.0))
            a_rk.append(jnp.where(incl2, ak[lc:], 0.0))
        yield
        zs = [jnp.concatenate([op[0], _mm(a, split(op[6]))], axis=1) for a, op in zip(a_ak, ops)]
        yield
        aps = list(a_ab)
        n_fac = int(math.log2(lc))
        for f in range(n_fac):
            zs = [z + _mm(ap, split(z)) for ap, z in zip(aps, zs)]
            if f + 1 < n_fac:
                aps = [_mm(ap, split(ap)) for ap in aps]
            yield
        for n, ((ci, p), (at, rt, bt, kt, bh, kh, vp)) in enumerate(zip(probs, ops)):
            ps = slice(p * PAIR_W, (p + 1) * PAIR_W)
            idx = i * RWKV_BLOCK + ci
            z = zs[n]
            gy = _mm(a_rb[n], split(z))
            g_mat = rt + gy[:, 0:PAIR_W]
            y0 = gy[:, PAIR_W:] + _mm(a_rk[n], split(vp))
            mc = _mm_tn(bh, z)
            kv = _mm_tn(kh, vp)
            w_chunk = w_chunk_all[ci * lc:ci * lc + 1, ps]
            m_mat = jnp.where(pair_bd, mc[:, 0:PAIR_W], 0.0) + jnp.where(pair_eye, w_chunk, 0.0)
            c_mat = jnp.where(pair_bd, mc[:, PAIR_W:] + kv, 0.0)
            gm_ref[idx, p, 0:lc, :] = g_mat.astype(BF16)
            gm_ref[idx, p, lc:, :] = m_mat.astype(BF16)
            yc_ref[idx, p, 0:lc, :] = y0
            yc_ref[idx, p, lc:, :] = c_mat

    def advance(i):
        for ci in range(RWKV_BLOCK):
            c = i * RWKV_BLOCK + ci
            t0 = pl.multiple_of(c * lc, lc)
            for p in range(N_HEADS // 2):
                step = jnp.dot(gm_ref[c, p], state_ref[p].astype(BF16), preferred_element_type=F32)
                step = step + yc_ref[c, p]
                rec_ref[pl.ds(t0, lc), p * PAIR_W:(p + 1) * PAIR_W] = step[0:lc]
                state_ref[p] = step[lc:]
            yield
        t0 = pl.multiple_of(i * tb, tb)
        o = rec_ref[pl.ds(t0, tb), :]
        mean = _head_sum(o, ones_bd) * (1.0 / HEAD_DIM)
        yield
        oc = o - mean
        var = _head_sum(oc * oc, ones_bd) * (1.0 / HEAD_DIM)
        yield
        o = oc * lax.rsqrt(var + GN_EPS) * lnw_ref[...] + lnb_ref[...]
        o = (o + bonus_ref[pl.ds(t0, tb), :]) * gate_ref[pl.ds(t0, tb), :]
        o_ref[0, pl.ds(t0, tb), :] = o.astype(BF16)

    def interleave(*streams):
        live = list(streams)
        while live:
            live = [s for s in live if next(s, live) is not live]

    n_blocks = seq // tb
    interleave(build(0))

    def body(i, carry):
        interleave(build(i), advance(i - 1))
        return carry

    lax.fori_loop(1, n_blocks, body, 0)
    interleave(advance(n_blocks - 1))


def _mix_c(pc, mu, w0, w_up, a0, a_up, g_up, k_k, k_a, r_k, ln_w, ln_b):
    b, s, _ = pc.shape
    g = GROUP_W
    n_pairs = N_HEADS // 2
    lora = jnp.zeros((LORA_W, 3 * g), F32)
    lora = lora.at[0:RANK_W, 0:g].set(w_up)
    lora = lora.at[RANK_W:RANK_W + RANK_A, g:2 * g].set(a_up)
    lora = lora.at[RANK_W + RANK_A:, 2 * g:].set(g_up)
    vec = lambda t: t.reshape(1, g)
    return pl.pallas_call(
        _rwkv_body,
        grid=(b,),
        in_specs=[pl.BlockSpec((1, s, C_COLS), lambda i: (i, 0, 0)), _resident((1, C_COLS)),
                  _resident((1, g)), _resident((LORA_W, 3 * g)), _resident((1, g)), _resident((1, g)),
                  _resident((1, g)), _resident((1, g)), _resident((1, g)), _resident((1, g))],
        out_specs=pl.BlockSpec((1, s, g), lambda i: (i, 0, 0)),
        out_shape=jax.ShapeDtypeStruct((b, s, g), BF16),
        scratch_shapes=[
            pltpu.VMEM((s // RWKV_CHUNK, n_pairs, RWKV_CHUNK + PAIR_W, PAIR_W), BF16),
            pltpu.VMEM((s // RWKV_CHUNK, n_pairs, RWKV_CHUNK + PAIR_W, PAIR_W), F32),
            pltpu.VMEM((s, g), F32),
            pltpu.VMEM((s, g), F32),
            pltpu.VMEM((s, g), F32),
            pltpu.VMEM((n_pairs, PAIR_W, PAIR_W), F32),
            pltpu.VMEM((1, C_COLS), F32),
        ],
        compiler_params=_params("parallel"),
        name="mix_c",
    )(pc, mu.reshape(1, C_COLS), vec(w0), lora.astype(BF16), vec(a0), vec(k_k), vec(k_a), vec(r_k),
      vec(ln_w), vec(ln_b))


def kernel(x, ffn1_pre_g, ffn1_w_gate, ffn1_w_up, ffn1_w_down, ffn1_post_g, mix_pre_g, w_in, sc_conv_w, sg_ln_w, sg_ln_b, sg_w, sg_b, rk_mu, rk_w0, rk_w_up, rk_a0, rk_a_up, rk_g_up, rk_k_k, rk_k_a, rk_r_k, rk_ln_w, rk_ln_b, cm_conv_w, cm_conv_b, cm_ln_w, cm_ln_b, w_out, mix_post_g, ffn2_pre_g, ffn2_w_gate, ffn2_w_up, ffn2_w_down, ffn2_post_g):
    b, s, d = x.shape
    t = b * s
    bf = lambda w: w.astype(BF16)
    xf = x.reshape(t, d)
    for l in range(ffn1_pre_g.shape[0]):
        xf = _ffn(xf, ffn1_pre_g[l], bf(ffn1_w_gate[l]), bf(ffn1_w_up[l]), bf(ffn1_w_down[l]),
                  ffn1_post_g[l])
        ya, yb, pc, yd = _mix_in(xf, s, mix_pre_g[l], bf(w_in[l]), sc_conv_w[l], sg_ln_w[l], sg_ln_b[l],
                                 sg_w[l], sg_b[l], cm_conv_w[l], cm_conv_b[l], cm_ln_w[l], cm_ln_b[l])
        yc = _mix_c(pc.reshape(b, s, C_COLS), rk_mu[l], rk_w0[l], rk_w_up[l], rk_a0[l], rk_a_up[l],
                    rk_g_up[l], rk_k_k[l], rk_k_a[l], rk_r_k[l].reshape(-1), rk_ln_w[l], rk_ln_b[l])
        xf = _ffn(xf, ffn2_pre_g[l], bf(ffn2_w_gate[l]), bf(ffn2_w_up[l]), bf(ffn2_w_down[l]),
                  ffn2_post_g[l],
                  mix=((ya, yb, yc.reshape(t, GROUP_W), yd), bf(w_out[l]), mix_post_g[l]))
    return xf.reshape(b, s, d)
```

```python
import functools
import math

import jax
import jax.numpy as jnp
from jax import lax
from jax.experimental import pallas as pl
from jax.experimental.pallas import tpu as pltpu

F32 = jnp.float32
BF16 = jnp.bfloat16

D_MODEL = 1024
D_FF = 2816
GROUP_W = 256
HEAD_DIM = 64
N_HEADS = 4
SC_WIDTH = 3
CHUNK = 128
CM_WIDTH = 31
RANK_W = 32
RANK_A = 32
RANK_G = 64
A_COLS = 3 * GROUP_W
B_COLS = 2 * GROUP_W
C_COLS = 3 * GROUP_W + RANK_W + RANK_A + RANK_G
D_COLS = 2 * GROUP_W
IN_COLS = A_COLS + B_COLS + C_COLS + D_COLS
LORA_W = RANK_W + RANK_A + RANK_G
RMS_EPS = 1e-6
LN_EPS = 1e-5
GN_EPS = 1e-5 * HEAD_DIM
FFN_RESID = 0.5

V7X_VMEM_BYTES = 64 * 1024 * 1024
VMEM_LIMIT_BYTES = V7X_VMEM_BYTES - 8 * 1024 * 1024
SUBLANES = 8

FFN_TILE = 512
SEQ_TILE = 512
FFN_CHUNK = 256
N_FFN_CHUNKS = D_FF // FFN_CHUNK
RWKV_CHUNK = 64
RWKV_BLOCK = 4
CONV_PAD = 32
CONV_TILE = 128
PAIR_W = 2 * HEAD_DIM


def _params(*sem):
    return pltpu.CompilerParams(dimension_semantics=sem, vmem_limit_bytes=VMEM_LIMIT_BYTES)


def _resident(shape):
    nd = len(shape)
    return pl.BlockSpec(shape, lambda *_: (0,) * nd, pipeline_mode=pl.Buffered(1))


def _mm(a, b):
    return jnp.dot(a.astype(BF16), b.astype(BF16), preferred_element_type=F32)


def _mm_nt(a, b):
    return lax.dot_general(a.astype(BF16), b.astype(BF16), (((1,), (1,)), ((), ())),
                           preferred_element_type=F32)


def _mm_tn(a, b):
    return lax.dot_general(a.astype(BF16), b.astype(BF16), (((0,), (0,)), ((), ())),
                           preferred_element_type=F32)


def _rms(x, g):
    return x * lax.rsqrt(jnp.mean(x * x, axis=-1, keepdims=True) + RMS_EPS) * g


def _layer_norm(x, g, b, eps):
    mu = jnp.mean(x, axis=-1, keepdims=True)
    xc = x - mu
    var = jnp.mean(xc * xc, axis=-1, keepdims=True)
    return xc * lax.rsqrt(var + eps) * g + b


def _ffn_tail(x, pre_g_ref, wg_ref, wu_ref, wd_ref, post_g_ref, o_ref, acc_ref):
    h = _rms(x, pre_g_ref[...]).astype(BF16)
    for c in range(D_FF // FFN_CHUNK):
        sl = slice(c * FFN_CHUNK, (c + 1) * FFN_CHUNK)
        g = jnp.dot(h, wg_ref[:, sl], preferred_element_type=F32)
        u = jnp.dot(h, wu_ref[:, sl], preferred_element_type=F32)
        act = (g * jax.nn.sigmoid(g) * u).astype(BF16)
        part = jnp.dot(act, wd_ref[sl, :], preferred_element_type=F32)
        if c == 0:
            acc_ref[...] = part
        else:
            acc_ref[...] += part
    o_ref[...] = x + FFN_RESID * _rms(acc_ref[...], post_g_ref[...])


def _ffn_phases(step, x_fn, pre_g_ref, wg32_ref, wu32_ref, wd32_ref, post_g_ref, o_ref,
                wg_ref, wu_ref, wd_ref, acc_ref):
    for c in range(N_FFN_CHUNKS):
        @pl.when(step == c)
        def _(c=c):
            sl = slice(c * FFN_CHUNK, (c + 1) * FFN_CHUNK)
            wg_ref[:, sl] = wg32_ref[0].astype(BF16)
            wu_ref[:, sl] = wu32_ref[0].astype(BF16)
            wd_ref[sl, :] = wd32_ref[0].astype(BF16)

    @pl.when(step >= N_FFN_CHUNKS)
    def _():
        _ffn_tail(x_fn(), pre_g_ref.at[0], wg_ref, wu_ref, wd_ref, post_g_ref.at[0], o_ref, acc_ref)


def _ffn_body(x_ref, *refs):
    _ffn_phases(pl.program_id(0), lambda: x_ref[...], *refs)


def _mix_out_ffn_body(x_ref, ya_ref, yb_ref, yc_ref, yd_ref, wo32_ref, mix_g_ref, *refs):
    wo_ref, ffn_refs = refs[-5], refs[:-5] + refs[-4:]
    step = pl.program_id(0)

    @pl.when(step == 0)
    def _():
        wo_ref[...] = wo32_ref[0].astype(BF16)

    def mixed():
        m = None
        for i, y_ref in enumerate((ya_ref, yb_ref, yc_ref, yd_ref)):
            part = jnp.dot(y_ref[...], wo_ref[i * GROUP_W:(i + 1) * GROUP_W, :], preferred_element_type=F32)
            m = part if m is None else m + part
        return x_ref[...] + _rms(m, mix_g_ref[0])

    _ffn_phases(step, mixed, *ffn_refs)


def _ffn(x, layer, pre_g, wg, wu, wd, post_g, mix=None):
    t, d = x.shape
    n_layers, _, f = wg.shape
    last = N_FFN_CHUNKS - 1
    tile = lambda width: pl.BlockSpec((FFN_TILE, width), lambda i: (jnp.maximum(i - N_FFN_CHUNKS, 0), 0))
    gain = pl.BlockSpec((1, 1, d), lambda i: (layer, 0, 0))
    gains = lambda g: g.reshape(n_layers, 1, d)
    ffn_specs = [gain,
                 pl.BlockSpec((1, d, FFN_CHUNK), lambda i: (layer, 0, jnp.minimum(i, last))),
                 pl.BlockSpec((1, d, FFN_CHUNK), lambda i: (layer, 0, jnp.minimum(i, last))),
                 pl.BlockSpec((1, FFN_CHUNK, d), lambda i: (layer, jnp.minimum(i, last), 0)),
                 gain]
    ffn_args = (gains(pre_g), wg, wu, wd, gains(post_g))
    ffn_scratch = [pltpu.VMEM((d, f), BF16), pltpu.VMEM((d, f), BF16), pltpu.VMEM((f, d), BF16),
                   pltpu.VMEM((FFN_TILE, d), F32)]
    if mix is None:
        body, specs, args, scratch = _ffn_body, [tile(d)], (x,), ffn_scratch
    else:
        ys, w_out, mix_g = mix
        body = _mix_out_ffn_body
        specs = ([tile(d)] + [tile(GROUP_W)] * len(ys)
                 + [pl.BlockSpec((1, d, d), lambda i: (layer, 0, 0), pipeline_mode=pl.Buffered(1)), gain])
        args = (x, *ys, w_out, gains(mix_g))
        scratch = [pltpu.VMEM((d, d), BF16)] + ffn_scratch
    return pl.pallas_call(
        body,
        grid=(N_FFN_CHUNKS + t // FFN_TILE,),
        in_specs=specs + ffn_specs,
        out_specs=tile(d),
        out_shape=jax.ShapeDtypeStruct((t, d), F32),
        scratch_shapes=scratch,
        compiler_params=_params("arbitrary"),
        name="ffn" if mix is None else "mix_out_ffn",
    )(*args, *ffn_args)


def _causal_conv_tile(zpad_ref, w_ref, width, t0):
    zh = zpad_ref[t0:t0 + CONV_TILE + CONV_PAD, :]
    acc = None
    for r in range(min(SUBLANES, width)):
        zr = zh if r == 0 else pltpu.roll(zh, r, 0)
        for q in range((width - 1 - r) // SUBLANES + 1):
            j = width - 1 - (SUBLANES * q + r)
            start = CONV_PAD - SUBLANES * q
            term = zr[start:start + CONV_TILE] * w_ref[j:j + 1, :]
            acc = term if acc is None else acc + term
    return acc


_OFF_B = A_COLS
_OFF_C = A_COLS + B_COLS
_OFF_D = A_COLS + B_COLS + C_COLS
IN_CHUNK = 384
N_IN_CHUNKS = IN_COLS // IN_CHUNK


def _mix_in_body(x_ref, g_ref, w32_ref, *refs, tiles_per_seq):
    w_ref = refs[-3]
    step = pl.program_id(0)
    for c in range(N_IN_CHUNKS):
        @pl.when(step == c)
        def _(c=c):
            w_ref[:, c * IN_CHUNK:(c + 1) * IN_CHUNK] = w32_ref[0].astype(BF16)

    @pl.when(step >= N_IN_CHUNKS)
    def _():
        first = (step - N_IN_CHUNKS) % tiles_per_seq == 0
        _mix_in_tile(first, x_ref, g_ref, w_ref, *refs[:-3], *refs[-2:])


def _mix_in_tile(first, x_ref, g_ref, w_ref, scw_ref, sg_lnw_ref, sg_lnb_ref, sgw_ref, sgb_ref,
                 cmw_ref, cmb_ref, cm_lnw_ref, cm_lnb_ref,
                 ya_ref, yb_ref, pc_ref, yd_ref, za_ref, zd_ref):
    g = GROUP_W
    ts = x_ref.shape[0]

    @pl.when(first)
    def _():
        za_ref[0:CONV_PAD, :] = jnp.zeros((CONV_PAD, g), F32)
        zd_ref[0:CONV_PAD, :] = jnp.zeros((CONV_PAD, g), F32)

    @pl.when(jnp.logical_not(first))
    def _():
        za_ref[0:CONV_PAD, :] = za_ref[ts:ts + CONV_PAD, :]
        zd_ref[0:CONV_PAD, :] = zd_ref[ts:ts + CONV_PAD, :]

    h = _rms(x_ref[...], g_ref[...]).astype(BF16)
    pd = jnp.dot(h, w_ref[:, _OFF_D:_OFF_D + D_COLS], preferred_element_type=F32)
    zd_ref[CONV_PAD:, :] = pd[:, 0:g] * jax.nn.sigmoid(pd[:, g:2 * g])
    pa = jnp.dot(h, w_ref[:, 0:A_COLS], preferred_element_type=F32)
    za_ref[CONV_PAD:, :] = pa[:, g:2 * g] * pa[:, 2 * g:3 * g]
    pc_ref[...] = jnp.dot(h, w_ref[:, _OFF_C:_OFF_C + C_COLS], preferred_element_type=F32)
    pb = jnp.dot(h, w_ref[:, _OFF_B:_OFF_B + B_COLS], preferred_element_type=F32)

    for t0 in range(0, ts, CONV_TILE):
        rows = slice(t0, t0 + CONV_TILE)
        y = _causal_conv_tile(zd_ref, cmw_ref, CM_WIDTH, t0) + cmb_ref[...]
        y = _layer_norm(y, cm_lnw_ref[...], cm_lnb_ref[...], LN_EPS)
        yd_ref[rows, :] = (y * jax.nn.sigmoid(y)).astype(BF16)
        ya_ref[rows, :] = (pa[rows, 0:g] * _causal_conv_tile(za_ref, scw_ref, SC_WIDTH, t0)).astype(BF16)

    row = lax.broadcasted_iota(jnp.int32, (CHUNK, CHUNK), 0)
    col = lax.broadcasted_iota(jnp.int32, (CHUNK, CHUNK), 1)
    w_tril = [jnp.where(row >= col, sgw_ref[hd], 0.0).astype(BF16) for hd in range(N_HEADS)]
    lane_head = lax.broadcasted_iota(jnp.int32, (CHUNK, g), 1) // HEAD_DIM
    for t0 in range(0, ts, CHUNK):
        rows = slice(t0, t0 + CHUNK)
        v = _layer_norm(pb[rows, g:2 * g], sg_lnw_ref[...], sg_lnb_ref[...], LN_EPS).astype(BF16)
        s = sgb_ref[...]
        for hd in range(N_HEADS):
            sh = jnp.dot(w_tril[hd], v, preferred_element_type=F32)
            s = s + jnp.where(lane_head == hd, sh, 0.0)
        yb_ref[rows, :] = (pb[rows, 0:g] * s).astype(BF16)


def _mix_in(x, seq, layer, g, w_in, sc_conv_w, sg_ln_w, sg_ln_b, sg_w, sg_b, cm_conv_w, cm_conv_b, cm_ln_w,
            cm_ln_b):
    t, d = x.shape
    gw = GROUP_W
    vec = lambda v: v.reshape(1, gw)
    sg_bias = jnp.repeat(sg_b.T, HEAD_DIM, axis=1)
    tile = lambda width: pl.BlockSpec((SEQ_TILE, width), lambda i: (jnp.maximum(i - N_IN_CHUNKS, 0), 0))
    w_chunk = pl.BlockSpec((1, d, IN_CHUNK), lambda i: (layer, 0, jnp.minimum(i, N_IN_CHUNKS - 1)))
    return pl.pallas_call(
        functools.partial(_mix_in_body, tiles_per_seq=seq // SEQ_TILE),
        grid=(N_IN_CHUNKS + t // SEQ_TILE,),
        in_specs=[tile(d), _resident((1, d)), w_chunk, _resident((SC_WIDTH, gw)),
                  _resident((1, gw)), _resident((1, gw)), _resident((N_HEADS, CHUNK, CHUNK)),
                  _resident((CHUNK, gw)), _resident((CM_WIDTH, gw)), _resident((1, gw)),
                  _resident((1, gw)), _resident((1, gw))],
        out_specs=[tile(gw), tile(gw), tile(C_COLS), tile(gw)],
        out_shape=[jax.ShapeDtypeStruct((t, gw), BF16), jax.ShapeDtypeStruct((t, gw), BF16),
                   jax.ShapeDtypeStruct((t, C_COLS), F32), jax.ShapeDtypeStruct((t, gw), BF16)],
        scratch_shapes=[pltpu.VMEM((d, IN_COLS), BF16), pltpu.VMEM((SEQ_TILE + CONV_PAD, gw), F32),
                        pltpu.VMEM((SEQ_TILE + CONV_PAD, gw), F32)],
        compiler_params=_params("arbitrary"),
        name="mix_in",
    )(x, g.reshape(1, d), w_in, sc_conv_w, vec(sg_ln_w), vec(sg_ln_b), sg_w, sg_bias,
      cm_conv_w, vec(cm_conv_b), vec(cm_ln_w), vec(cm_ln_b))


def _split2(x):
    hi = x.astype(BF16)
    lo = (x - hi.astype(F32)).astype(BF16)
    return hi, lo


def _head_sum(x, ones_bd):
    hi, lo = _split2(x)
    return (jnp.dot(hi, ones_bd, preferred_element_type=F32)
            + jnp.dot(lo, ones_bd, preferred_element_type=F32))


def _rwkv_body(pc_ref, mu_ref, w0_ref, lora_ref, a0_ref, kk_ref, ka_ref, rk_ref, lnw_ref, lnb_ref,
               o_ref, gm_ref, yc_ref, rec_ref, bonus_ref, gate_ref, state_ref, prev_ref):
    g = GROUP_W
    lc = RWKV_CHUNK
    tb = RWKV_BLOCK * lc
    seq = pc_ref.shape[1]

    row = lax.broadcasted_iota(jnp.int32, (lc, 2 * lc), 0)
    col = lax.broadcasted_iota(jnp.int32, (lc, 2 * lc), 1) % lc
    strict2 = row > col
    incl2 = row >= col
    brow = lax.broadcasted_iota(jnp.int32, (tb, tb), 0)
    bcol = lax.broadcasted_iota(jnp.int32, (tb, tb), 1)
    tri_ones = jnp.where((brow // lc == bcol // lc) & (brow >= bcol), 1.0, 0.0).astype(BF16)
    bd_r = lax.broadcasted_iota(jnp.int32, (g, g), 0) // HEAD_DIM
    bd_c = lax.broadcasted_iota(jnp.int32, (g, g), 1) // HEAD_DIM
    ones_bd = jnp.where(bd_r == bd_c, 1.0, 0.0).astype(BF16)
    pr = lax.broadcasted_iota(jnp.int32, (PAIR_W, PAIR_W), 0)
    pcol = lax.broadcasted_iota(jnp.int32, (PAIR_W, PAIR_W), 1)
    pair_bd = (pr // HEAD_DIM) == (pcol // HEAD_DIM)
    pair_eye = pr == pcol
    first_head = lax.broadcasted_iota(jnp.int32, (lc, PAIR_W), 1) < HEAD_DIM
    first_head2 = jnp.concatenate([first_head, first_head], axis=1)
    row0 = lax.broadcasted_iota(jnp.int32, (tb, C_COLS), 0) == 0

    def split(y):
        yb = y.astype(BF16)
        m = first_head if y.shape[1] == PAIR_W else first_head2
        zero = jnp.zeros_like(yb)
        return jnp.concatenate([jnp.where(m, yb, zero), jnp.where(m, zero, yb)], axis=0)

    state_ref[...] = jnp.zeros(state_ref.shape, F32)
    prev_ref[...] = jnp.zeros(prev_ref.shape, F32)

    def build(i):
        t0 = pl.multiple_of(i * tb, tb)
        x = pc_ref[0, pl.ds(t0, tb), :]
        xs = jnp.where(row0, prev_ref[...], pltpu.roll(x, 1, 0))
        prev_ref[...] = x[tb - 1:tb, :]
        xm = x + (xs - x) * mu_ref[...]
        r = xm[:, 0:g]
        k = xm[:, g:2 * g]
        v = xm[:, 2 * g:3 * g]
        lo_in = xm[:, 3 * g:3 * g + LORA_W]
        lane = lax.broadcasted_iota(jnp.int32, (tb, LORA_W), 1)
        lo_act = jnp.where(lane < RANK_W, jnp.tanh(lo_in),
                           jnp.where(lane < RANK_W + RANK_A, lo_in, jax.nn.sigmoid(lo_in)))
        lora = _mm(lo_act, lora_ref[...])
        e = jax.nn.sigmoid(w0_ref[...] + lora[:, 0:g]) * math.exp(-0.5)
        a_sig = jax.nn.sigmoid(a0_ref[...] + lora[:, g:2 * g])
        gate_ref[pl.ds(t0, tb), :] = lora[:, 2 * g:3 * g]
        kk = k * kk_ref[...]
        kk = kk / jnp.maximum(jnp.sqrt(_head_sum(kk * kk, ones_bd)), 1e-12)
        k = k * (1.0 + (a_sig - 1.0) * ka_ref[...])
        a_ = -kk
        b_ = kk * a_sig
        bonus_ref[pl.ds(t0, tb), :] = _head_sum(r * k * rk_ref[...], ones_bd) * v

        e_hi = e.astype(BF16)
        e_r = e - e_hi.astype(F32)
        e_mid = e_r.astype(BF16)
        e_lo = (e_r - e_mid.astype(F32)).astype(BF16)
        cs = (jnp.dot(tri_ones, e_hi, preferred_element_type=F32)
              + jnp.dot(tri_ones, e_mid, preferred_element_type=F32)
              + jnp.dot(tri_ones, e_lo, preferred_element_type=F32))
        cs_last = jnp.concatenate(
            [jnp.broadcast_to(cs[(ci + 1) * lc - 1:(ci + 1) * lc, :], (lc, g)) for ci in range(RWKV_BLOCK)],
            axis=0)
        w_inc = jnp.exp(-cs)
        w_exc = jnp.exp(e - cs)
        w_inv = jnp.exp(cs)
        w_fin = jnp.exp(cs - cs_last)
        w_chunk_all = jnp.exp(-cs_last)
        at_all = a_ * w_exc
        rt_all = r * w_inc
        bt_all = b_ * w_inv
        kt_all = k * w_inv
        bh_all = b_ * w_fin
        kh_all = k * w_fin
        yield

        probs = [(ci, p) for ci in range(RWKV_BLOCK) for p in range(N_HEADS // 2)]
        ops = []
        for ci, p in probs:
            rs = slice(ci * lc, (ci + 1) * lc)
            ps = slice(p * PAIR_W, (p + 1) * PAIR_W)
            ops.append(tuple(t[rs, ps] for t in (at_all, rt_all, bt_all, kt_all, bh_all, kh_all, v)))
        a_ab, a_ak, a_rb, a_rk = [], [], [], []
        for at, rt, bt, kt, bh, kh, vp in ops:
            lhs = jnp.concatenate([at, rt], axis=0)
            ab = _mm_nt(lhs, split(bt))
            ak = _mm_nt(lhs, split(kt))
            a_ab.append(jnp.where(strict2, ab[0:lc], 0.0))
            a_ak.append(jnp.where(strict2, ak[0:lc], 0.0))
            a_rb.append(jnp.where(incl2, ab[lc:], 0.0))
            a_rk.append(jnp.where(incl2, ak[lc:], 0.0))
        yield
        zs = [jnp.concatenate([op[0], _mm(a, split(op[6]))], axis=1) for a, op in zip(a_ak, ops)]
        yield
        aps = list(a_ab)
        n_fac = int(math.log2(lc))
        for f in range(n_fac):
            zs = [z + _mm(ap, split(z)) for ap, z in zip(aps, zs)]
            if f + 1 < n_fac:
                aps = [_mm(ap, split(ap)) for ap in aps]
            yield
        for n, ((ci, p), (at, rt, bt, kt, bh, kh, vp)) in enumerate(zip(probs, ops)):
            ps = slice(p * PAIR_W, (p + 1) * PAIR_W)
            idx = i * RWKV_BLOCK + ci
            z = zs[n]
            gy = _mm(a_rb[n], split(z))
            g_mat = rt + gy[:, 0:PAIR_W]
            y0 = gy[:, PAIR_W:] + _mm(a_rk[n], split(vp))
            mc = _mm_tn(bh, z)
            kv = _mm_tn(kh, vp)
            w_chunk = w_chunk_all[ci * lc:ci * lc + 1, ps]
            m_mat = jnp.where(pair_bd, mc[:, 0:PAIR_W], 0.0) + jnp.where(pair_eye, w_chunk, 0.0)
            c_mat = jnp.where(pair_bd, mc[:, PAIR_W:] + kv, 0.0)
            gm_ref[idx, p, 0:lc, :] = g_mat.astype(BF16)
            gm_ref[idx, p, lc:, :] = m_mat.astype(BF16)
            yc_ref[idx, p, 0:lc, :] = y0
            yc_ref[idx, p, lc:, :] = c_mat

    def advance(i):
        for ci in range(RWKV_BLOCK):
            c = i * RWKV_BLOCK + ci
            t0 = pl.multiple_of(c * lc, lc)
            for p in range(N_HEADS // 2):
                step = jnp.dot(gm_ref[c, p], state_ref[p].astype(BF16), preferred_element_type=F32)
                step = step + yc_ref[c, p]
                rec_ref[pl.ds(t0, lc), p * PAIR_W:(p + 1) * PAIR_W] = step[0:lc]
                state_ref[p] = step[lc:]
            yield
        t0 = pl.multiple_of(i * tb, tb)
        o = rec_ref[pl.ds(t0, tb), :]
        mean = _head_sum(o, ones_bd) * (1.0 / HEAD_DIM)
        yield
        oc = o - mean
        var = _head_sum(oc * oc, ones_bd) * (1.0 / HEAD_DIM)
        yield
        o = oc * lax.rsqrt(var + GN_EPS) * lnw_ref[...] + lnb_ref[...]
        o = (o + bonus_ref[pl.ds(t0, tb), :]) * gate_ref[pl.ds(t0, tb), :]
        o_ref[0, pl.ds(t0, tb), :] = o.astype(BF16)

    def interleave(*streams):
        live = list(streams)
        while live:
            live = [s for s in live if next(s, live) is not live]

    n_blocks = seq // tb
    interleave(build(0))

    def body(i, carry):
        interleave(build(i), advance(i - 1))
        return carry

    lax.fori_loop(1, n_blocks, body, 0)
    interleave(advance(n_blocks - 1))


def _mix_c(pc, mu, w0, w_up, a0, a_up, g_up, k_k, k_a, r_k, ln_w, ln_b):
    b, s, _ = pc.shape
    g = GROUP_W
    n_pairs = N_HEADS // 2
    lora = jnp.zeros((LORA_W, 3 * g), F32)
    lora = lora.at[0:RANK_W, 0:g].set(w_up)
    lora = lora.at[RANK_W:RANK_W + RANK_A, g:2 * g].set(a_up)
    lora = lora.at[RANK_W + RANK_A:, 2 * g:].set(g_up)
    vec = lambda t: t.reshape(1, g)
    return pl.pallas_call(
        _rwkv_body,
        grid=(b,),
        in_specs=[pl.BlockSpec((1, s, C_COLS), lambda i: (i, 0, 0)), _resident((1, C_COLS)),
                  _resident((1, g)), _resident((LORA_W, 3 * g)), _resident((1, g)), _resident((1, g)),
                  _resident((1, g)), _resident((1, g)), _resident((1, g)), _resident((1, g))],
        out_specs=pl.BlockSpec((1, s, g), lambda i: (i, 0, 0)),
        out_shape=jax.ShapeDtypeStruct((b, s, g), BF16),
        scratch_shapes=[
            pltpu.VMEM((s // RWKV_CHUNK, n_pairs, RWKV_CHUNK + PAIR_W, PAIR_W), BF16),
            pltpu.VMEM((s // RWKV_CHUNK, n_pairs, RWKV_CHUNK + PAIR_W, PAIR_W), F32),
            pltpu.VMEM((s, g), F32),
            pltpu.VMEM((s, g), F32),
            pltpu.VMEM((s, g), F32),
            pltpu.VMEM((n_pairs, PAIR_W, PAIR_W), F32),
            pltpu.VMEM((1, C_COLS), F32),
        ],
        compiler_params=_params("parallel"),
        name="mix_c",
    )(pc, mu.reshape(1, C_COLS), vec(w0), lora.astype(BF16), vec(a0), vec(k_k), vec(k_a), vec(r_k),
      vec(ln_w), vec(ln_b))


def kernel(x, ffn1_pre_g, ffn1_w_gate, ffn1_w_up, ffn1_w_down, ffn1_post_g, mix_pre_g, w_in, sc_conv_w, sg_ln_w, sg_ln_b, sg_w, sg_b, rk_mu, rk_w0, rk_w_up, rk_a0, rk_a_up, rk_g_up, rk_k_k, rk_k_a, rk_r_k, rk_ln_w, rk_ln_b, cm_conv_w, cm_conv_b, cm_ln_w, cm_ln_b, w_out, mix_post_g, ffn2_pre_g, ffn2_w_gate, ffn2_w_up, ffn2_w_down, ffn2_post_g):
    b, s, d = x.shape
    t = b * s
    xf = x.reshape(t, d)
    for l in range(ffn1_pre_g.shape[0]):
        xf = _ffn(xf, l, ffn1_pre_g, ffn1_w_gate, ffn1_w_up, ffn1_w_down, ffn1_post_g)
        ya, yb, pc, yd = _mix_in(xf, s, l, mix_pre_g[l], w_in, sc_conv_w[l], sg_ln_w[l], sg_ln_b[l],
                                 sg_w[l], sg_b[l], cm_conv_w[l], cm_conv_b[l], cm_ln_w[l], cm_ln_b[l])
        yc = _mix_c(pc.reshape(b, s, C_COLS), rk_mu[l], rk_w0[l], rk_w_up[l], rk_a0[l], rk_a_up[l],
                    rk_g_up[l], rk_k_k[l], rk_k_a[l], rk_r_k[l].reshape(-1), rk_ln_w[l], rk_ln_b[l])
        xf = _ffn(xf, l, ffn2_pre_g, ffn2_w_gate, ffn2_w_up, ffn2_w_down, ffn2_post_g,
                  mix=((ya, yb, yc.reshape(t, GROUP_W), yd), w_out, mix_post_g))
    return xf.reshape(b, s, d)
```

```python
import functools
import math

import jax
import jax.numpy as jnp
from jax import lax
from jax.experimental import pallas as pl
from jax.experimental.pallas import tpu as pltpu

F32 = jnp.float32
BF16 = jnp.bfloat16

D_MODEL = 1024
D_FF = 2816
GROUP_W = 256
HEAD_DIM = 64
N_HEADS = 4
SC_WIDTH = 3
CHUNK = 128
CM_WIDTH = 31
RANK_W = 32
RANK_A = 32
RANK_G = 64
A_COLS = 3 * GROUP_W
B_COLS = 2 * GROUP_W
C_COLS = 3 * GROUP_W + RANK_W + RANK_A + RANK_G
D_COLS = 2 * GROUP_W
IN_COLS = A_COLS + B_COLS + C_COLS + D_COLS
LORA_W = RANK_W + RANK_A + RANK_G
RMS_EPS = 1e-6
LN_EPS = 1e-5
GN_EPS = 1e-5 * HEAD_DIM
FFN_RESID = 0.5

V7X_VMEM_BYTES = 64 * 1024 * 1024
VMEM_LIMIT_BYTES = V7X_VMEM_BYTES - 8 * 1024 * 1024
SUBLANES = 8

FFN_TILE = 512
SEQ_TILE = 512
FFN_CHUNK = 256
N_FFN_CHUNKS = D_FF // FFN_CHUNK
RWKV_CHUNK = 64
RWKV_BLOCK = 4
CONV_PAD = 32
CONV_TILE = 128
PAIR_W = 2 * HEAD_DIM
assert CONV_TILE == CHUNK and CONV_PAD >= CM_WIDTH - 1 and CONV_PAD % SUBLANES == 0


def _params(*sem):
    return pltpu.CompilerParams(dimension_semantics=sem, vmem_limit_bytes=VMEM_LIMIT_BYTES)


def _resident(shape):
    nd = len(shape)
    return pl.BlockSpec(shape, lambda *_: (0,) * nd, pipeline_mode=pl.Buffered(1))


def _mm(a, b):
    return jnp.dot(a.astype(BF16), b.astype(BF16), preferred_element_type=F32)


def _mm_nt(a, b):
    return lax.dot_general(a.astype(BF16), b.astype(BF16), (((1,), (1,)), ((), ())),
                           preferred_element_type=F32)


def _mm_tn(a, b):
    return lax.dot_general(a.astype(BF16), b.astype(BF16), (((0,), (0,)), ((), ())),
                           preferred_element_type=F32)


def _sigmoid(x):
    return 0.5 * jnp.tanh(0.5 * x) + 0.5


def _rms(x, g):
    return x * lax.rsqrt(jnp.mean(x * x, axis=-1, keepdims=True) + RMS_EPS) * g


def _layer_norm(x, g, b, eps):
    mu = jnp.mean(x, axis=-1, keepdims=True)
    xc = x - mu
    var = jnp.mean(xc * xc, axis=-1, keepdims=True)
    return xc * lax.rsqrt(var + eps) * g + b


def _ffn_tail(x, pre_g_ref, wg_ref, wu_ref, wd_ref, post_g_ref, o_ref, acc_ref):
    h = _rms(x, pre_g_ref[...]).astype(BF16)
    for c in range(D_FF // FFN_CHUNK):
        sl = slice(c * FFN_CHUNK, (c + 1) * FFN_CHUNK)
        g = jnp.dot(h, wg_ref[:, sl], preferred_element_type=F32)
        u = jnp.dot(h, wu_ref[:, sl], preferred_element_type=F32)
        act = (g * _sigmoid(g) * u).astype(BF16)
        part = jnp.dot(act, wd_ref[sl, :], preferred_element_type=F32)
        if c == 0:
            acc_ref[...] = part
        else:
            acc_ref[...] += part
    o_ref[...] = x + FFN_RESID * _rms(acc_ref[...], post_g_ref[...])


def _ffn_phases(step, x_fn, pre_g_ref, wg32_ref, wu32_ref, wd32_ref, post_g_ref, o_ref,
                wg_ref, wu_ref, wd_ref, acc_ref):
    for c in range(N_FFN_CHUNKS):
        @pl.when(step == c)
        def _(c=c):
            sl = slice(c * FFN_CHUNK, (c + 1) * FFN_CHUNK)
            wg_ref[:, sl] = wg32_ref[0].astype(BF16)
            wu_ref[:, sl] = wu32_ref[0].astype(BF16)
            wd_ref[sl, :] = wd32_ref[0].astype(BF16)

    @pl.when(step >= N_FFN_CHUNKS)
    def _():
        _ffn_tail(x_fn(), pre_g_ref.at[0], wg_ref, wu_ref, wd_ref, post_g_ref.at[0], o_ref, acc_ref)


def _ffn_body(x_ref, *refs):
    _ffn_phases(pl.program_id(0), lambda: x_ref[...], *refs)


def _mix_out_ffn_body(x_ref, ya_ref, yb_ref, yc_ref, yd_ref, wo32_ref, mix_g_ref, *refs):
    wo_ref, ffn_refs = refs[-5], refs[:-5] + refs[-4:]
    step = pl.program_id(0)

    @pl.when(step == 0)
    def _():
        wo_ref[...] = wo32_ref[0].astype(BF16)

    def mixed():
        m = None
        for i, y_ref in enumerate((ya_ref, yb_ref, yc_ref, yd_ref)):
            part = jnp.dot(y_ref[...], wo_ref[i * GROUP_W:(i + 1) * GROUP_W, :], preferred_element_type=F32)
            m = part if m is None else m + part
        return x_ref[...] + _rms(m, mix_g_ref[0])

    _ffn_phases(step, mixed, *ffn_refs)


def _ffn(x, layer, pre_g, wg, wu, wd, post_g, mix=None):
    t, d = x.shape
    n_layers, _, f = wg.shape
    last = N_FFN_CHUNKS - 1
    tile = lambda width: pl.BlockSpec((FFN_TILE, width), lambda i: (jnp.maximum(i - N_FFN_CHUNKS, 0), 0))
    gain = pl.BlockSpec((1, 1, d), lambda i: (layer, 0, 0))
    gains = lambda g: g.reshape(n_layers, 1, d)
    ffn_specs = [gain,
                 pl.BlockSpec((1, d, FFN_CHUNK), lambda i: (layer, 0, jnp.minimum(i, last))),
                 pl.BlockSpec((1, d, FFN_CHUNK), lambda i: (layer, 0, jnp.minimum(i, last))),
                 pl.BlockSpec((1, FFN_CHUNK, d), lambda i: (layer, jnp.minimum(i, last), 0)),
                 gain]
    ffn_args = (gains(pre_g), wg, wu, wd, gains(post_g))
    ffn_scratch = [pltpu.VMEM((d, f), BF16), pltpu.VMEM((d, f), BF16), pltpu.VMEM((f, d), BF16),
                   pltpu.VMEM((FFN_TILE, d), F32)]
    if mix is None:
        body, specs, args, scratch = _ffn_body, [tile(d)], (x,), ffn_scratch
    else:
        ys, w_out, mix_g = mix
        body = _mix_out_ffn_body
        specs = ([tile(d)] + [tile(GROUP_W)] * len(ys)
                 + [pl.BlockSpec((1, d, d), lambda i: (layer, 0, 0), pipeline_mode=pl.Buffered(1)), gain])
        args = (x, *ys, w_out, gains(mix_g))
        scratch = [pltpu.VMEM((d, d), BF16)] + ffn_scratch
    return pl.pallas_call(
        body,
        grid=(N_FFN_CHUNKS + t // FFN_TILE,),
        in_specs=specs + ffn_specs,
        out_specs=tile(d),
        out_shape=jax.ShapeDtypeStruct((t, d), F32),
        scratch_shapes=scratch,
        compiler_params=_params("arbitrary"),
        name="ffn" if mix is None else "mix_out_ffn",
    )(*args, *ffn_args)


def _causal_conv_tile(zpad_ref, w_ref, width, t0):
    zh = zpad_ref[t0:t0 + CONV_TILE + CONV_PAD, :]
    acc = None
    for r in range(min(SUBLANES, width)):
        zr = zh if r == 0 else pltpu.roll(zh, r, 0)
        for q in range((width - 1 - r) // SUBLANES + 1):
            j = width - 1 - (SUBLANES * q + r)
            start = CONV_PAD - SUBLANES * q
            term = zr[start:start + CONV_TILE] * w_ref[j:j + 1, :]
            acc = term if acc is None else acc + term
    return acc


_OFF_B = A_COLS
_OFF_C = A_COLS + B_COLS
_OFF_D = A_COLS + B_COLS + C_COLS
IN_CHUNK = 384
N_IN_CHUNKS = IN_COLS // IN_CHUNK


def _mix_in_body(x_ref, g_ref, w32_ref, *refs, tiles_per_seq):
    w_ref = refs[-3]
    step = pl.program_id(0)
    for c in range(N_IN_CHUNKS):
        @pl.when(step == c)
        def _(c=c):
            w_ref[:, c * IN_CHUNK:(c + 1) * IN_CHUNK] = w32_ref[0].astype(BF16)

    @pl.when(step >= N_IN_CHUNKS)
    def _():
        first = (step - N_IN_CHUNKS) % tiles_per_seq == 0
        _mix_in_tile(first, x_ref, g_ref, w_ref, *refs[:-3], *refs[-2:])


def _mix_in_tile(first, x_ref, g_ref, w_ref, scw_ref, sg_lnw_ref, sg_lnb_ref, sgw_ref, sgb_ref,
                 cmw_ref, cmb_ref, cm_lnw_ref, cm_lnb_ref,
                 ya_ref, yb_ref, pc_ref, yd_ref, za_ref, zd_ref):
    g = GROUP_W
    ts = x_ref.shape[0]

    @pl.when(first)
    def _():
        za_ref[0:CONV_PAD, :] = jnp.zeros((CONV_PAD, g), F32)
        zd_ref[0:CONV_PAD, :] = jnp.zeros((CONV_PAD, g), F32)

    @pl.when(jnp.logical_not(first))
    def _():
        za_ref[0:CONV_PAD, :] = za_ref[ts:ts + CONV_PAD, :]
        zd_ref[0:CONV_PAD, :] = zd_ref[ts:ts + CONV_PAD, :]

    h = _rms(x_ref[...], g_ref[...]).astype(BF16)
    z1 = jnp.dot(h, w_ref[:, _OFF_D:_OFF_D + g], preferred_element_type=F32)
    z2 = jnp.dot(h, w_ref[:, _OFF_D + g:_OFF_D + 2 * g], preferred_element_type=F32)
    zd_ref[CONV_PAD:, :] = z1 * _sigmoid(z2)
    pa = jnp.dot(h, w_ref[:, 0:A_COLS], preferred_element_type=F32)
    za_ref[CONV_PAD:, :] = pa[:, g:2 * g] * pa[:, 2 * g:3 * g]
    pb = jnp.dot(h, w_ref[:, _OFF_B:_OFF_B + B_COLS], preferred_element_type=F32)
    pc_ref[...] = jnp.dot(h, w_ref[:, _OFF_C:_OFF_C + C_COLS], preferred_element_type=F32)

    for t0 in range(0, ts, CONV_TILE):
        rows = slice(t0, t0 + CONV_TILE)
        y = _causal_conv_tile(zd_ref, cmw_ref, CM_WIDTH, t0) + cmb_ref[...]
        y = _layer_norm(y, cm_lnw_ref[...], cm_lnb_ref[...], LN_EPS)
        yd_ref[rows, :] = (y * _sigmoid(y)).astype(BF16)
        ya_ref[rows, :] = (pa[rows, 0:g] * _causal_conv_tile(za_ref, scw_ref, SC_WIDTH, t0)).astype(BF16)

    row = lax.broadcasted_iota(jnp.int32, (CHUNK, CHUNK), 0)
    col = lax.broadcasted_iota(jnp.int32, (CHUNK, CHUNK), 1)
    w_tril = [jnp.where(row >= col, sgw_ref[hd], 0.0).astype(BF16) for hd in range(N_HEADS)]
    lane_head = lax.broadcasted_iota(jnp.int32, (CHUNK, g), 1) // HEAD_DIM
    for t0 in range(0, ts, CHUNK):
        rows = slice(t0, t0 + CHUNK)
        v = _layer_norm(pb[rows, g:2 * g], sg_lnw_ref[...], sg_lnb_ref[...], LN_EPS).astype(BF16)
        s = sgb_ref[...]
        for hd in range(N_HEADS):
            sh = jnp.dot(w_tril[hd], v, preferred_element_type=F32)
            s = s + jnp.where(lane_head == hd, sh, 0.0)
        yb_ref[rows, :] = (pb[rows, 0:g] * s).astype(BF16)


def _mix_in(x, seq, layer, g, w_in, sc_conv_w, sg_ln_w, sg_ln_b, sg_w, sg_b, cm_conv_w, cm_conv_b, cm_ln_w,
            cm_ln_b):
    t, d = x.shape
    gw = GROUP_W
    vec = lambda v: v.reshape(1, gw)
    sg_bias = jnp.repeat(sg_b.T, HEAD_DIM, axis=1)
    tile = lambda width: pl.BlockSpec((SEQ_TILE, width), lambda i: (jnp.maximum(i - N_IN_CHUNKS, 0), 0))
    w_chunk = pl.BlockSpec((1, d, IN_CHUNK), lambda i: (layer, 0, jnp.minimum(i, N_IN_CHUNKS - 1)))
    return pl.pallas_call(
        functools.partial(_mix_in_body, tiles_per_seq=seq // SEQ_TILE),
        grid=(N_IN_CHUNKS + t // SEQ_TILE,),
        in_specs=[tile(d), _resident((1, d)), w_chunk, _resident((SC_WIDTH, gw)),
                  _resident((1, gw)), _resident((1, gw)), _resident((N_HEADS, CHUNK, CHUNK)),
                  _resident((CHUNK, gw)), _resident((CM_WIDTH, gw)), _resident((1, gw)),
                  _resident((1, gw)), _resident((1, gw))],
        out_specs=[tile(gw), tile(gw), tile(C_COLS), tile(gw)],
        out_shape=[jax.ShapeDtypeStruct((t, gw), BF16), jax.ShapeDtypeStruct((t, gw), BF16),
                   jax.ShapeDtypeStruct((t, C_COLS), F32), jax.ShapeDtypeStruct((t, gw), BF16)],
        scratch_shapes=[pltpu.VMEM((d, IN_COLS), BF16), pltpu.VMEM((SEQ_TILE + CONV_PAD, gw), F32),
                        pltpu.VMEM((SEQ_TILE + CONV_PAD, gw), F32)],
        compiler_params=_params("arbitrary"),
        name="mix_in",
    )(x, g.reshape(1, d), w_in, sc_conv_w, vec(sg_ln_w), vec(sg_ln_b), sg_w, sg_bias,
      cm_conv_w, vec(cm_conv_b), vec(cm_ln_w), vec(cm_ln_b))


def _split2(x):
    hi = x.astype(BF16)
    lo = (x - hi.astype(F32)).astype(BF16)
    return hi, lo


def _head_sum(x, ones_bd):
    hi, lo = _split2(x)
    return (jnp.dot(hi, ones_bd, preferred_element_type=F32)
            + jnp.dot(lo, ones_bd, preferred_element_type=F32))


def _rwkv_body(pc_ref, mu_ref, w0_ref, lora_ref, a0_ref, kk_ref, ka_ref, rk_ref, lnw_ref, lnb_ref,
               o_ref, gm_ref, yc_ref, rec_ref, bonus_ref, gate_ref, state_ref, prev_ref):
    g = GROUP_W
    lc = RWKV_CHUNK
    tb = RWKV_BLOCK * lc
    seq = pc_ref.shape[1]

    row = lax.broadcasted_iota(jnp.int32, (lc, 2 * lc), 0)
    col = lax.broadcasted_iota(jnp.int32, (lc, 2 * lc), 1) % lc
    strict2 = row > col
    incl2 = row >= col
    brow = lax.broadcasted_iota(jnp.int32, (tb, tb), 0)
    bcol = lax.broadcasted_iota(jnp.int32, (tb, tb), 1)
    tri_ones = jnp.where((brow // lc == bcol // lc) & (brow >= bcol), 1.0, 0.0).astype(BF16)
    bd_r = lax.broadcasted_iota(jnp.int32, (g, g), 0) // HEAD_DIM
    bd_c = lax.broadcasted_iota(jnp.int32, (g, g), 1) // HEAD_DIM
    ones_bd = jnp.where(bd_r == bd_c, 1.0, 0.0).astype(BF16)
    pr = lax.broadcasted_iota(jnp.int32, (PAIR_W, PAIR_W), 0)
    pcol = lax.broadcasted_iota(jnp.int32, (PAIR_W, PAIR_W), 1)
    pair_bd = (pr // HEAD_DIM) == (pcol // HEAD_DIM)
    pair_eye = pr == pcol
    first_head = lax.broadcasted_iota(jnp.int32, (lc, PAIR_W), 1) < HEAD_DIM
    first_head2 = jnp.concatenate([first_head, first_head], axis=1)
    row0 = lax.broadcasted_iota(jnp.int32, (tb, C_COLS), 0) == 0

    def split(y):
        yb = y.astype(BF16)
        m = first_head if y.shape[1] == PAIR_W else first_head2
        zero = jnp.zeros_like(yb)
        return jnp.concatenate([jnp.where(m, yb, zero), jnp.where(m, zero, yb)], axis=0)

    state_ref[...] = jnp.zeros(state_ref.shape, F32)
    prev_ref[...] = jnp.zeros(prev_ref.shape, F32)

    def build(i):
        t0 = pl.multiple_of(i * tb, tb)
        x = pc_ref[0, pl.ds(t0, tb), :]
        xs = jnp.where(row0, prev_ref[...], pltpu.roll(x, 1, 0))
        prev_ref[...] = x[tb - 1:tb, :]
        xm = x + (xs - x) * mu_ref[...]
        r = xm[:, 0:g]
        k = xm[:, g:2 * g]
        v = xm[:, 2 * g:3 * g]
        lo_in = xm[:, 3 * g:3 * g + LORA_W]
        lane = lax.broadcasted_iota(jnp.int32, (tb, LORA_W), 1)
        lo_act = jnp.where(lane < RANK_W, jnp.tanh(lo_in),
                           jnp.where(lane < RANK_W + RANK_A, lo_in, _sigmoid(lo_in)))
        lora = _mm(lo_act, lora_ref[...])
        e = _sigmoid(w0_ref[...] + lora[:, 0:g]) * math.exp(-0.5)
        a_sig = _sigmoid(a0_ref[...] + lora[:, g:2 * g])
        gate_ref[pl.ds(t0, tb), :] = lora[:, 2 * g:3 * g]
        kk = k * kk_ref[...]
        kk = kk * jnp.minimum(lax.rsqrt(_head_sum(kk * kk, ones_bd)), 1e12)
        k = k * (1.0 + (a_sig - 1.0) * ka_ref[...])
        a_ = -kk
        b_ = kk * a_sig
        bonus_ref[pl.ds(t0, tb), :] = _head_sum(r * k * rk_ref[...], ones_bd) * v

        e_hi = e.astype(BF16)
        e_r = e - e_hi.astype(F32)
        e_mid = e_r.astype(BF16)
        e_lo = (e_r - e_mid.astype(F32)).astype(BF16)
        cs = (jnp.dot(tri_ones, e_hi, preferred_element_type=F32)
              + jnp.dot(tri_ones, e_mid, preferred_element_type=F32)
              + jnp.dot(tri_ones, e_lo, preferred_element_type=F32))
        cs_last = jnp.concatenate(
            [jnp.broadcast_to(cs[(ci + 1) * lc - 1:(ci + 1) * lc, :], (lc, g)) for ci in range(RWKV_BLOCK)],
            axis=0)
        w_inc = jnp.exp(-cs)
        w_exc = jnp.exp(e - cs)
        w_inv = jnp.exp(cs)
        w_fin = jnp.exp(cs - cs_last)
        w_chunk_all = jnp.exp(-cs_last)
        at_all = a_ * w_exc
        rt_all = r * w_inc
        bt_all = b_ * w_inv
        kt_all = k * w_inv
        bh_all = b_ * w_fin
        kh_all = k * w_fin
        yield

        probs = [(ci, p) for ci in range(RWKV_BLOCK) for p in range(N_HEADS // 2)]
        ops = []
        for ci, p in probs:
            rs = slice(ci * lc, (ci + 1) * lc)
            ps = slice(p * PAIR_W, (p + 1) * PAIR_W)
            ops.append(tuple(t[rs, ps] for t in (at_all, rt_all, bt_all, kt_all, bh_all, kh_all, v)))
        a_ab, a_ak, a_rb, a_rk = [], [], [], []
        for at, rt, bt, kt, bh, kh, vp in ops:
            lhs = jnp.concatenate([at, rt], axis=0)
            ab = _mm_nt(lhs, split(bt))
            ak = _mm_nt(lhs, split(kt))
            a_ab.append(jnp.where(strict2, ab[0:lc], 0.0))
            a_ak.append(jnp.where(strict2, ak[0:lc], 0.0))
            a_rb.append(jnp.where(incl2, ab[lc:], 0.0))
            a_rk.append(jnp.where(incl2, ak[lc:], 0.0))
        yield
        zs = [jnp.concatenate([op[0], _mm(a, split(op[6]))], axis=1) for a, op in zip(a_ak, ops)]
        yield
        aps = list(a_ab)
        n_fac = int(math.log2(lc))
        for f in range(n_fac):
            zs = [z + _mm(ap, split(z)) for ap, z in zip(aps, zs)]
            if f + 1 < n_fac:
                aps = [_mm(ap, split(ap)) for ap in aps]
            yield
        for n, ((ci, p), (at, rt, bt, kt, bh, kh, vp)) in enumerate(zip(probs, ops)):
            ps = slice(p * PAIR_W, (p + 1) * PAIR_W)
            idx = i * RWKV_BLOCK + ci
            z = zs[n]
            gy = _mm(a_rb[n], split(z))
            g_mat = rt + gy[:, 0:PAIR_W]
            y0 = gy[:, PAIR_W:] + _mm(a_rk[n], split(vp))
            mc = _mm_tn(bh, z)
            kv = _mm_tn(kh, vp)
            w_chunk = w_chunk_all[ci * lc:ci * lc + 1, ps]
            m_mat = jnp.where(pair_bd, mc[:, 0:PAIR_W], 0.0) + jnp.where(pair_eye, w_chunk, 0.0)
            c_mat = jnp.where(pair_bd, mc[:, PAIR_W:] + kv, 0.0)
            gm_ref[idx, p, 0:lc, :] = g_mat.astype(BF16)
            gm_ref[idx, p, lc:, :] = m_mat.astype(BF16)
            yc_ref[idx, p, 0:lc, :] = y0
            yc_ref[idx, p, lc:, :] = c_mat

    def advance(i):
        for ci in range(RWKV_BLOCK):
            c = i * RWKV_BLOCK + ci
            t0 = pl.multiple_of(c * lc, lc)
            for p in range(N_HEADS // 2):
                step = jnp.dot(gm_ref[c, p], state_ref[p].astype(BF16), preferred_element_type=F32)
                step = step + yc_ref[c, p]
                rec_ref[pl.ds(t0, lc), p * PAIR_W:(p + 1) * PAIR_W] = step[0:lc]
                state_ref[p] = step[lc:]
            yield
        t0 = pl.multiple_of(i * tb, tb)
        o = rec_ref[pl.ds(t0, tb), :]
        mean = _head_sum(o, ones_bd) * (1.0 / HEAD_DIM)
        yield
        oc = o - mean
        var = _head_sum(oc * oc, ones_bd) * (1.0 / HEAD_DIM)
        yield
        o = oc * lax.rsqrt(var + GN_EPS) * lnw_ref[...] + lnb_ref[...]
        o = (o + bonus_ref[pl.ds(t0, tb), :]) * gate_ref[pl.ds(t0, tb), :]
        o_ref[0, pl.ds(t0, tb), :] = o.astype(BF16)

    def interleave(*streams):
        live = list(streams)
        while live:
            live = [s for s in live if next(s, live) is not live]

    n_blocks = seq // tb
    interleave(build(0))

    def body(i, carry):
        interleave(build(i), advance(i - 1))
        return carry

    lax.fori_loop(1, n_blocks, body, 0)
    interleave(advance(n_blocks - 1))


def _mix_c(pc, mu, w0, w_up, a0, a_up, g_up, k_k, k_a, r_k, ln_w, ln_b):
    b, s, _ = pc.shape
    g = GROUP_W
    n_pairs = N_HEADS // 2
    lora = jnp.zeros((LORA_W, 3 * g), F32)
    lora = lora.at[0:RANK_W, 0:g].set(w_up)
    lora = lora.at[RANK_W:RANK_W + RANK_A, g:2 * g].set(a_up)
    lora = lora.at[RANK_W + RANK_A:, 2 * g:].set(g_up)
    vec = lambda t: t.reshape(1, g)
    return pl.pallas_call(
        _rwkv_body,
        grid=(b,),
        in_specs=[pl.BlockSpec((1, s, C_COLS), lambda i: (i, 0, 0)), _resident((1, C_COLS)),
                  _resident((1, g)), _resident((LORA_W, 3 * g)), _resident((1, g)), _resident((1, g)),
                  _resident((1, g)), _resident((1, g)), _resident((1, g)), _resident((1, g))],
        out_specs=pl.BlockSpec((1, s, g), lambda i: (i, 0, 0)),
        out_shape=jax.ShapeDtypeStruct((b, s, g), BF16),
        scratch_shapes=[
            pltpu.VMEM((s // RWKV_CHUNK, n_pairs, RWKV_CHUNK + PAIR_W, PAIR_W), BF16),
            pltpu.VMEM((s // RWKV_CHUNK, n_pairs, RWKV_CHUNK + PAIR_W, PAIR_W), F32),
            pltpu.VMEM((s, g), F32),
            pltpu.VMEM((s, g), F32),
            pltpu.VMEM((s, g), F32),
            pltpu.VMEM((n_pairs, PAIR_W, PAIR_W), F32),
            pltpu.VMEM((1, C_COLS), F32),
        ],
        compiler_params=_params("parallel"),
        name="mix_c",
    )(pc, mu.reshape(1, C_COLS), vec(w0), lora.astype(BF16), vec(a0), vec(k_k), vec(k_a), vec(r_k),
      vec(ln_w), vec(ln_b))


def kernel(x, ffn1_pre_g, ffn1_w_gate, ffn1_w_up, ffn1_w_down, ffn1_post_g, mix_pre_g, w_in, sc_conv_w, sg_ln_w, sg_ln_b, sg_w, sg_b, rk_mu, rk_w0, rk_w_up, rk_a0, rk_a_up, rk_g_up, rk_k_k, rk_k_a, rk_r_k, rk_ln_w, rk_ln_b, cm_conv_w, cm_conv_b, cm_ln_w, cm_ln_b, w_out, mix_post_g, ffn2_pre_g, ffn2_w_gate, ffn2_w_up, ffn2_w_down, ffn2_post_g):
    b, s, d = x.shape
    t = b * s
    xf = x.reshape(t, d)
    for l in range(ffn1_pre_g.shape[0]):
        xf = _ffn(xf, l, ffn1_pre_g, ffn1_w_gate, ffn1_w_up, ffn1_w_down, ffn1_post_g)
        ya, yb, pc, yd = _mix_in(xf, s, l, mix_pre_g[l], w_in, sc_conv_w[l], sg_ln_w[l], sg_ln_b[l],
                                 sg_w[l], sg_b[l], cm_conv_w[l], cm_conv_b[l], cm_ln_w[l], cm_ln_b[l])
        yc = _mix_c(pc.reshape(b, s, C_COLS), rk_mu[l], rk_w0[l], rk_w_up[l], rk_a0[l], rk_a_up[l],
                    rk_g_up[l], rk_k_k[l], rk_k_a[l], rk_r_k[l].reshape(-1), rk_ln_w[l], rk_ln_b[l])
        xf = _ffn(xf, l, ffn2_pre_g, ffn2_w_gate, ffn2_w_up, ffn2_w_down, ffn2_post_g,
                  mix=((ya, yb, yc.reshape(t, GROUP_W), yd), w_out, mix_post_g))
    return xf.reshape(b, s, d)
```

```python
import functools
import math

import jax
import jax.numpy as jnp
from jax import lax
from jax.experimental import pallas as pl
from jax.experimental.pallas import tpu as pltpu

F32 = jnp.float32
BF16 = jnp.bfloat16

D_MODEL = 1024
D_FF = 2816
GROUP_W = 256
HEAD_DIM = 64
N_HEADS = 4
SC_WIDTH = 3
CHUNK = 128
CM_WIDTH = 31
RANK_W = 32
RANK_A = 32
RANK_G = 64
A_COLS = 3 * GROUP_W
B_COLS = 2 * GROUP_W
C_COLS = 3 * GROUP_W + RANK_W + RANK_A + RANK_G
D_COLS = 2 * GROUP_W
IN_COLS = A_COLS + B_COLS + C_COLS + D_COLS
LORA_W = RANK_W + RANK_A + RANK_G
RMS_EPS = 1e-6
LN_EPS = 1e-5
GN_EPS = 1e-5 * HEAD_DIM
FFN_RESID = 0.5

V7X_VMEM_BYTES = 64 * 1024 * 1024
VMEM_LIMIT_BYTES = V7X_VMEM_BYTES - 8 * 1024 * 1024
SUBLANES = 8

FFN_TILE = 512
SEQ_TILE = 1024
FFN_CHUNK = 256
N_FFN_CHUNKS = D_FF // FFN_CHUNK
RWKV_CHUNK = 64
RWKV_BLOCK = 4
CONV_PAD = 32
CONV_TILE = 128
PAIR_W = 2 * HEAD_DIM
assert CONV_TILE == CHUNK and CONV_PAD >= CM_WIDTH - 1 and CONV_PAD % SUBLANES == 0


def _params(*sem):
    return pltpu.CompilerParams(dimension_semantics=sem, vmem_limit_bytes=VMEM_LIMIT_BYTES)


def _resident(shape):
    nd = len(shape)
    return pl.BlockSpec(shape, lambda *_: (0,) * nd, pipeline_mode=pl.Buffered(1))


def _mm(a, b):
    return jnp.dot(a.astype(BF16), b.astype(BF16), preferred_element_type=F32)


def _mm_nt(a, b):
    return lax.dot_general(a.astype(BF16), b.astype(BF16), (((1,), (1,)), ((), ())),
                           preferred_element_type=F32)


def _mm_tn(a, b):
    return lax.dot_general(a.astype(BF16), b.astype(BF16), (((0,), (0,)), ((), ())),
                           preferred_element_type=F32)


def _sigmoid(x):
    return 0.5 * jnp.tanh(0.5 * x) + 0.5


def _rms(x, g):
    return x * lax.rsqrt(jnp.mean(x * x, axis=-1, keepdims=True) + RMS_EPS) * g


def _layer_norm(x, g, b, eps):
    mu = jnp.mean(x, axis=-1, keepdims=True)
    xc = x - mu
    var = jnp.mean(xc * xc, axis=-1, keepdims=True)
    return xc * lax.rsqrt(var + eps) * g + b


def _ffn_tail(x, pre_g_ref, wg_ref, wu_ref, wd_ref, post_g_ref, o_ref, acc_ref):
    h = _rms(x, pre_g_ref[...]).astype(BF16)
    for c in range(D_FF // FFN_CHUNK):
        sl = slice(c * FFN_CHUNK, (c + 1) * FFN_CHUNK)
        g = jnp.dot(h, wg_ref[:, sl], preferred_element_type=F32)
        u = jnp.dot(h, wu_ref[:, sl], preferred_element_type=F32)
        act = (g * _sigmoid(g) * u).astype(BF16)
        part = jnp.dot(act, wd_ref[sl, :], preferred_element_type=F32)
        if c == 0:
            acc_ref[...] = part
        else:
            acc_ref[...] += part
    o_ref[...] = x + FFN_RESID * _rms(acc_ref[...], post_g_ref[...])


def _ffn_phases(step, x_fn, pre_g_ref, wg32_ref, wu32_ref, wd32_ref, post_g_ref, o_ref,
                wg_ref, wu_ref, wd_ref, acc_ref):
    for c in range(N_FFN_CHUNKS):
        @pl.when(step == c)
        def _(c=c):
            sl = slice(c * FFN_CHUNK, (c + 1) * FFN_CHUNK)
            wg_ref[:, sl] = wg32_ref[0].astype(BF16)
            wu_ref[:, sl] = wu32_ref[0].astype(BF16)
            wd_ref[sl, :] = wd32_ref[0].astype(BF16)

    @pl.when(step >= N_FFN_CHUNKS)
    def _():
        _ffn_tail(x_fn(), pre_g_ref.at[0], wg_ref, wu_ref, wd_ref, post_g_ref.at[0], o_ref, acc_ref)


def _ffn_body(x_ref, *refs):
    _ffn_phases(pl.program_id(0), lambda: x_ref[...], *refs)


def _mix_out_ffn_body(x_ref, ya_ref, yb_ref, yc_ref, yd_ref, wo32_ref, mix_g_ref, *refs):
    wo_ref, ffn_refs = refs[-5], refs[:-5] + refs[-4:]
    step = pl.program_id(0)

    @pl.when(step == 0)
    def _():
        wo_ref[...] = wo32_ref[0].astype(BF16)

    def mixed():
        m = None
        for i, y_ref in enumerate((ya_ref, yb_ref, yc_ref, yd_ref)):
            part = jnp.dot(y_ref[...], wo_ref[i * GROUP_W:(i + 1) * GROUP_W, :], preferred_element_type=F32)
            m = part if m is None else m + part
        return x_ref[...] + _rms(m, mix_g_ref[0])

    _ffn_phases(step, mixed, *ffn_refs)


def _ffn(x, layer, pre_g, wg, wu, wd, post_g, mix=None):
    t, d = x.shape
    n_layers, _, f = wg.shape
    last = N_FFN_CHUNKS - 1
    tile = lambda width: pl.BlockSpec((FFN_TILE, width), lambda i: (jnp.maximum(i - N_FFN_CHUNKS, 0), 0))
    gain = pl.BlockSpec((1, 1, d), lambda i: (layer, 0, 0))
    gains = lambda g: g.reshape(n_layers, 1, d)
    ffn_specs = [gain,
                 pl.BlockSpec((1, d, FFN_CHUNK), lambda i: (layer, 0, jnp.minimum(i, last))),
                 pl.BlockSpec((1, d, FFN_CHUNK), lambda i: (layer, 0, jnp.minimum(i, last))),
                 pl.BlockSpec((1, FFN_CHUNK, d), lambda i: (layer, jnp.minimum(i, last), 0)),
                 gain]
    ffn_args = (gains(pre_g), wg, wu, wd, gains(post_g))
    ffn_scratch = [pltpu.VMEM((d, f), BF16), pltpu.VMEM((d, f), BF16), pltpu.VMEM((f, d), BF16),
                   pltpu.VMEM((FFN_TILE, d), F32)]
    if mix is None:
        body, specs, args, scratch = _ffn_body, [tile(d)], (x,), ffn_scratch
    else:
        ys, w_out, mix_g = mix
        body = _mix_out_ffn_body
        specs = ([tile(d)] + [tile(GROUP_W)] * len(ys)
                 + [pl.BlockSpec((1, d, d), lambda i: (layer, 0, 0), pipeline_mode=pl.Buffered(1)), gain])
        args = (x, *ys, w_out, gains(mix_g))
        scratch = [pltpu.VMEM((d, d), BF16)] + ffn_scratch
    return pl.pallas_call(
        body,
        grid=(N_FFN_CHUNKS + t // FFN_TILE,),
        in_specs=specs + ffn_specs,
        out_specs=tile(d),
        out_shape=jax.ShapeDtypeStruct((t, d), F32),
        scratch_shapes=scratch,
        compiler_params=_params("arbitrary"),
        name="ffn" if mix is None else "mix_out_ffn",
    )(*args, *ffn_args)


def _causal_conv_tile(zpad_ref, w_ref, width, t0):
    zh = zpad_ref[t0:t0 + CONV_TILE + CONV_PAD, :]
    acc = None
    for r in range(min(SUBLANES, width)):
        zr = zh if r == 0 else pltpu.roll(zh, r, 0)
        for q in range((width - 1 - r) // SUBLANES + 1):
            j = width - 1 - (SUBLANES * q + r)
            start = CONV_PAD - SUBLANES * q
            term = zr[start:start + CONV_TILE] * w_ref[j:j + 1, :]
            acc = term if acc is None else acc + term
    return acc


_OFF_B = A_COLS
_OFF_C = A_COLS + B_COLS
_OFF_D = A_COLS + B_COLS + C_COLS
IN_CHUNK = 384
N_IN_CHUNKS = IN_COLS // IN_CHUNK


def _mix_in_body(x_ref, g_ref, w32_ref, *refs, tiles_per_seq):
    w_ref = refs[-3]
    step = pl.program_id(0)
    for c in range(N_IN_CHUNKS):
        @pl.when(step == c)
        def _(c=c):
            w_ref[:, c * IN_CHUNK:(c + 1) * IN_CHUNK] = w32_ref[0].astype(BF16)

    @pl.when(step >= N_IN_CHUNKS)
    def _():
        first = (step - N_IN_CHUNKS) % tiles_per_seq == 0
        _mix_in_tile(first, x_ref, g_ref, w_ref, *refs[:-3], *refs[-2:])


def _mix_in_tile(first, x_ref, g_ref, w_ref, scw_ref, sg_lnw_ref, sg_lnb_ref, sgw_ref, sgb_ref,
                 cmw_ref, cmb_ref, cm_lnw_ref, cm_lnb_ref,
                 ya_ref, yb_ref, pc_ref, yd_ref, za_ref, zd_ref):
    g = GROUP_W
    ts = x_ref.shape[0]

    @pl.when(first)
    def _():
        za_ref[0:CONV_PAD, :] = jnp.zeros((CONV_PAD, g), F32)
        zd_ref[0:CONV_PAD, :] = jnp.zeros((CONV_PAD, g), F32)

    @pl.when(jnp.logical_not(first))
    def _():
        za_ref[0:CONV_PAD, :] = za_ref[ts:ts + CONV_PAD, :]
        zd_ref[0:CONV_PAD, :] = zd_ref[ts:ts + CONV_PAD, :]

    h = _rms(x_ref[...], g_ref[...]).astype(BF16)
    z1 = jnp.dot(h, w_ref[:, _OFF_D:_OFF_D + g], preferred_element_type=F32)
    z2 = jnp.dot(h, w_ref[:, _OFF_D + g:_OFF_D + 2 * g], preferred_element_type=F32)
    zd_ref[CONV_PAD:, :] = z1 * _sigmoid(z2)
    pa = jnp.dot(h, w_ref[:, 0:A_COLS], preferred_element_type=F32)
    za_ref[CONV_PAD:, :] = pa[:, g:2 * g] * pa[:, 2 * g:3 * g]
    pb = jnp.dot(h, w_ref[:, _OFF_B:_OFF_B + B_COLS], preferred_element_type=F32)
    pc_ref[...] = jnp.dot(h, w_ref[:, _OFF_C:_OFF_C + C_COLS], preferred_element_type=F32)

    for t0 in range(0, ts, CONV_TILE):
        rows = slice(t0, t0 + CONV_TILE)
        y = _causal_conv_tile(zd_ref, cmw_ref, CM_WIDTH, t0) + cmb_ref[...]
        y = _layer_norm(y, cm_lnw_ref[...], cm_lnb_ref[...], LN_EPS)
        yd_ref[rows, :] = (y * _sigmoid(y)).astype(BF16)
        ya_ref[rows, :] = (pa[rows, 0:g] * _causal_conv_tile(za_ref, scw_ref, SC_WIDTH, t0)).astype(BF16)

    row = lax.broadcasted_iota(jnp.int32, (CHUNK, CHUNK), 0)
    col = lax.broadcasted_iota(jnp.int32, (CHUNK, CHUNK), 1)
    w_tril = [jnp.where(row >= col, sgw_ref[hd], 0.0).astype(BF16) for hd in range(N_HEADS)]
    lane_head = lax.broadcasted_iota(jnp.int32, (CHUNK, g), 1) // HEAD_DIM
    for t0 in range(0, ts, CHUNK):
        rows = slice(t0, t0 + CHUNK)
        v = _layer_norm(pb[rows, g:2 * g], sg_lnw_ref[...], sg_lnb_ref[...], LN_EPS).astype(BF16)
        s = sgb_ref[...]
        for hd in range(N_HEADS):
            sh = jnp.dot(w_tril[hd], v, preferred_element_type=F32)
            s = s + jnp.where(lane_head == hd, sh, 0.0)
        yb_ref[rows, :] = (pb[rows, 0:g] * s).astype(BF16)


def _mix_in(x, seq, layer, g, w_in, sc_conv_w, sg_ln_w, sg_ln_b, sg_w, sg_b, cm_conv_w, cm_conv_b, cm_ln_w,
            cm_ln_b):
    t, d = x.shape
    gw = GROUP_W
    vec = lambda v: v.reshape(1, gw)
    sg_bias = jnp.repeat(sg_b.T, HEAD_DIM, axis=1)
    tile = lambda width: pl.BlockSpec((SEQ_TILE, width), lambda i: (jnp.maximum(i - N_IN_CHUNKS, 0), 0))
    w_chunk = pl.BlockSpec((1, d, IN_CHUNK), lambda i: (layer, 0, jnp.minimum(i, N_IN_CHUNKS - 1)))
    return pl.pallas_call(
        functools.partial(_mix_in_body, tiles_per_seq=seq // SEQ_TILE),
        grid=(N_IN_CHUNKS + t // SEQ_TILE,),
        in_specs=[tile(d), _resident((1, d)), w_chunk, _resident((SC_WIDTH, gw)),
                  _resident((1, gw)), _resident((1, gw)), _resident((N_HEADS, CHUNK, CHUNK)),
                  _resident((CHUNK, gw)), _resident((CM_WIDTH, gw)), _resident((1, gw)),
                  _resident((1, gw)), _resident((1, gw))],
        out_specs=[tile(gw), tile(gw), tile(C_COLS), tile(gw)],
        out_shape=[jax.ShapeDtypeStruct((t, gw), BF16), jax.ShapeDtypeStruct((t, gw), BF16),
                   jax.ShapeDtypeStruct((t, C_COLS), F32), jax.ShapeDtypeStruct((t, gw), BF16)],
        scratch_shapes=[pltpu.VMEM((d, IN_COLS), BF16), pltpu.VMEM((SEQ_TILE + CONV_PAD, gw), F32),
                        pltpu.VMEM((SEQ_TILE + CONV_PAD, gw), F32)],
        compiler_params=_params("arbitrary"),
        name="mix_in",
    )(x, g.reshape(1, d), w_in, sc_conv_w, vec(sg_ln_w), vec(sg_ln_b), sg_w, sg_bias,
      cm_conv_w, vec(cm_conv_b), vec(cm_ln_w), vec(cm_ln_b))


def _split2(x):
    hi = x.astype(BF16)
    lo = (x - hi.astype(F32)).astype(BF16)
    return hi, lo


def _head_sum(x, ones_bd):
    hi, lo = _split2(x)
    return (jnp.dot(hi, ones_bd, preferred_element_type=F32)
            + jnp.dot(lo, ones_bd, preferred_element_type=F32))


def _rwkv_body(pc_ref, mu_ref, w0_ref, lora_ref, a0_ref, kk_ref, ka_ref, rk_ref, lnw_ref, lnb_ref,
               o_ref, gm_ref, yc_ref, rec_ref, bonus_ref, gate_ref, state_ref, prev_ref):
    g = GROUP_W
    lc = RWKV_CHUNK
    tb = RWKV_BLOCK * lc
    seq = pc_ref.shape[1]

    row = lax.broadcasted_iota(jnp.int32, (lc, 2 * lc), 0)
    col = lax.broadcasted_iota(jnp.int32, (lc, 2 * lc), 1) % lc
    strict2 = row > col
    incl2 = row >= col
    brow = lax.broadcasted_iota(jnp.int32, (tb, tb), 0)
    bcol = lax.broadcasted_iota(jnp.int32, (tb, tb), 1)
    tri_ones = jnp.where((brow // lc == bcol // lc) & (brow >= bcol), 1.0, 0.0).astype(BF16)
    bd_r = lax.broadcasted_iota(jnp.int32, (g, g), 0) // HEAD_DIM
    bd_c = lax.broadcasted_iota(jnp.int32, (g, g), 1) // HEAD_DIM
    ones_bd = jnp.where(bd_r == bd_c, 1.0, 0.0).astype(BF16)
    pr = lax.broadcasted_iota(jnp.int32, (PAIR_W, PAIR_W), 0)
    pcol = lax.broadcasted_iota(jnp.int32, (PAIR_W, PAIR_W), 1)
    pair_bd = (pr // HEAD_DIM) == (pcol // HEAD_DIM)
    pair_eye = pr == pcol
    first_head = lax.broadcasted_iota(jnp.int32, (lc, PAIR_W), 1) < HEAD_DIM
    first_head2 = jnp.concatenate([first_head, first_head], axis=1)
    row0 = lax.broadcasted_iota(jnp.int32, (tb, C_COLS), 0) == 0

    def split(y):
        yb = y.astype(BF16)
        m = first_head if y.shape[1] == PAIR_W else first_head2
        zero = jnp.zeros_like(yb)
        return jnp.concatenate([jnp.where(m, yb, zero), jnp.where(m, zero, yb)], axis=0)

    state_ref[...] = jnp.zeros(state_ref.shape, F32)
    prev_ref[...] = jnp.zeros(prev_ref.shape, F32)

    def build(i):
        t0 = pl.multiple_of(i * tb, tb)
        x = pc_ref[0, pl.ds(t0, tb), :]
        xs = jnp.where(row0, prev_ref[...], pltpu.roll(x, 1, 0))
        prev_ref[...] = x[tb - 1:tb, :]
        xm = x + (xs - x) * mu_ref[...]
        r = xm[:, 0:g]
        k = xm[:, g:2 * g]
        v = xm[:, 2 * g:3 * g]
        lo_in = xm[:, 3 * g:3 * g + LORA_W]
        lane = lax.broadcasted_iota(jnp.int32, (tb, LORA_W), 1)
        lo_act = jnp.where(lane < RANK_W, jnp.tanh(lo_in),
                           jnp.where(lane < RANK_W + RANK_A, lo_in, _sigmoid(lo_in)))
        lora = _mm(lo_act, lora_ref[...])
        e = _sigmoid(w0_ref[...] + lora[:, 0:g]) * math.exp(-0.5)
        a_sig = _sigmoid(a0_ref[...] + lora[:, g:2 * g])
        gate_ref[pl.ds(t0, tb), :] = lora[:, 2 * g:3 * g]
        kk = k * kk_ref[...]
        kk = kk * jnp.minimum(lax.rsqrt(_head_sum(kk * kk, ones_bd)), 1e12)
        k = k * (1.0 + (a_sig - 1.0) * ka_ref[...])
        a_ = -kk
        b_ = kk * a_sig
        bonus_ref[pl.ds(t0, tb), :] = _head_sum(r * k * rk_ref[...], ones_bd) * v

        e_hi = e.astype(BF16)
        e_r = e - e_hi.astype(F32)
        e_mid = e_r.astype(BF16)
        e_lo = (e_r - e_mid.astype(F32)).astype(BF16)
        cs = (jnp.dot(tri_ones, e_hi, preferred_element_type=F32)
              + jnp.dot(tri_ones, e_mid, preferred_element_type=F32)
              + jnp.dot(tri_ones, e_lo, preferred_element_type=F32))
        cs_last = jnp.concatenate(
            [jnp.broadcast_to(cs[(ci + 1) * lc - 1:(ci + 1) * lc, :], (lc, g)) for ci in range(RWKV_BLOCK)],
            axis=0)
        w_inc = jnp.exp(-cs)
        w_exc = jnp.exp(e - cs)
        w_inv = jnp.exp(cs)
        w_fin = jnp.exp(cs - cs_last)
        w_chunk_all = jnp.exp(-cs_last)
        at_all = a_ * w_exc
        rt_all = r * w_inc
        bt_all = b_ * w_inv
        kt_all = k * w_inv
        bh_all = b_ * w_fin
        kh_all = k * w_fin
        yield

        probs = [(ci, p) for ci in range(RWKV_BLOCK) for p in range(N_HEADS // 2)]
        ops = []
        for ci, p in probs:
            rs = slice(ci * lc, (ci + 1) * lc)
            ps = slice(p * PAIR_W, (p + 1) * PAIR_W)
            ops.append(tuple(t[rs, ps] for t in (at_all, rt_all, bt_all, kt_all, bh_all, kh_all, v)))
        a_ab, a_ak, a_rb, a_rk = [], [], [], []
        for at, rt, bt, kt, bh, kh, vp in ops:
            lhs = jnp.concatenate([at, rt], axis=0)
            ab = _mm_nt(lhs, split(bt))
            ak = _mm_nt(lhs, split(kt))
            a_ab.append(jnp.where(strict2, ab[0:lc], 0.0))
            a_ak.append(jnp.where(strict2, ak[0:lc], 0.0))
            a_rb.append(jnp.where(incl2, ab[lc:], 0.0))
            a_rk.append(jnp.where(incl2, ak[lc:], 0.0))
        yield
        zs = [jnp.concatenate([op[0], _mm(a, split(op[6]))], axis=1) for a, op in zip(a_ak, ops)]
        yield
        aps = list(a_ab)
        n_fac = int(math.log2(lc))
        for f in range(n_fac):
            zs = [z + _mm(ap, split(z)) for ap, z in zip(aps, zs)]
            if f + 1 < n_fac:
                aps = [_mm(ap, split(ap)) for ap in aps]
            yield
        for n, ((ci, p), (at, rt, bt, kt, bh, kh, vp)) in enumerate(zip(probs, ops)):
            ps = slice(p * PAIR_W, (p + 1) * PAIR_W)
            idx = i * RWKV_BLOCK + ci
            z = zs[n]
            gy = _mm(a_rb[n], split(z))
            g_mat = rt + gy[:, 0:PAIR_W]
            y0 = gy[:, PAIR_W:] + _mm(a_rk[n], split(vp))
            mc = _mm_tn(bh, z)
            kv = _mm_tn(kh, vp)
            w_chunk = w_chunk_all[ci * lc:ci * lc + 1, ps]
            m_mat = jnp.where(pair_bd, mc[:, 0:PAIR_W], 0.0) + jnp.where(pair_eye, w_chunk, 0.0)
            c_mat = jnp.where(pair_bd, mc[:, PAIR_W:] + kv, 0.0)
            gm_ref[idx, p, 0:lc, :] = g_mat.astype(BF16)
            gm_ref[idx, p, lc:, :] = m_mat.astype(BF16)
            yc_ref[idx, p, 0:lc, :] = y0
            yc_ref[idx, p, lc:, :] = c_mat

    def advance(i):
        for ci in range(RWKV_BLOCK):
            c = i * RWKV_BLOCK + ci
            t0 = pl.multiple_of(c * lc, lc)
            for p in range(N_HEADS // 2):
                step = jnp.dot(gm_ref[c, p], state_ref[p].astype(BF16), preferred_element_type=F32)
                step = step + yc_ref[c, p]
                rec_ref[pl.ds(t0, lc), p * PAIR_W:(p + 1) * PAIR_W] = step[0:lc]
                state_ref[p] = step[lc:]
            yield
        t0 = pl.multiple_of(i * tb, tb)
        o = rec_ref[pl.ds(t0, tb), :]
        mean = _head_sum(o, ones_bd) * (1.0 / HEAD_DIM)
        yield
        oc = o - mean
        var = _head_sum(oc * oc, ones_bd) * (1.0 / HEAD_DIM)
        yield
        o = oc * lax.rsqrt(var + GN_EPS) * lnw_ref[...] + lnb_ref[...]
        o = (o + bonus_ref[pl.ds(t0, tb), :]) * gate_ref[pl.ds(t0, tb), :]
        o_ref[0, pl.ds(t0, tb), :] = o.astype(BF16)

    def interleave(*streams):
        live = list(streams)
        while live:
            live = [s for s in live if next(s, live) is not live]

    n_blocks = seq // tb
    interleave(build(0))

    def body(i, carry):
        interleave(build(i), advance(i - 1))
        return carry

    lax.fori_loop(1, n_blocks, body, 0)
    interleave(advance(n_blocks - 1))


def _mix_c(pc, mu, w0, w_up, a0, a_up, g_up, k_k, k_a, r_k, ln_w, ln_b):
    b, s, _ = pc.shape
    g = GROUP_W
    n_pairs = N_HEADS // 2
    lora = jnp.zeros((LORA_W, 3 * g), F32)
    lora = lora.at[0:RANK_W, 0:g].set(w_up)
    lora = lora.at[RANK_W:RANK_W + RANK_A, g:2 * g].set(a_up)
    lora = lora.at[RANK_W + RANK_A:, 2 * g:].set(g_up)
    vec = lambda t: t.reshape(1, g)
    return pl.pallas_call(
        _rwkv_body,
        grid=(b,),
        in_specs=[pl.BlockSpec((1, s, C_COLS), lambda i: (i, 0, 0)), _resident((1, C_COLS)),
                  _resident((1, g)), _resident((LORA_W, 3 * g)), _resident((1, g)), _resident((1, g)),
                  _resident((1, g)), _resident((1, g)), _resident((1, g)), _resident((1, g))],
        out_specs=pl.BlockSpec((1, s, g), lambda i: (i, 0, 0)),
        out_shape=jax.ShapeDtypeStruct((b, s, g), BF16),
        scratch_shapes=[
            pltpu.VMEM((s // RWKV_CHUNK, n_pairs, RWKV_CHUNK + PAIR_W, PAIR_W), BF16),
            pltpu.VMEM((s // RWKV_CHUNK, n_pairs, RWKV_CHUNK + PAIR_W, PAIR_W), F32),
            pltpu.VMEM((s, g), F32),
            pltpu.VMEM((s, g), F32),
            pltpu.VMEM((s, g), F32),
            pltpu.VMEM((n_pairs, PAIR_W, PAIR_W), F32),
            pltpu.VMEM((1, C_COLS), F32),
        ],
        compiler_params=_params("parallel"),
        name="mix_c",
    )(pc, mu.reshape(1, C_COLS), vec(w0), lora.astype(BF16), vec(a0), vec(k_k), vec(k_a), vec(r_k),
      vec(ln_w), vec(ln_b))


def kernel(x, ffn1_pre_g, ffn1_w_gate, ffn1_w_up, ffn1_w_down, ffn1_post_g, mix_pre_g, w_in, sc_conv_w, sg_ln_w, sg_ln_b, sg_w, sg_b, rk_mu, rk_w0, rk_w_up, rk_a0, rk_a_up, rk_g_up, rk_k_k, rk_k_a, rk_r_k, rk_ln_w, rk_ln_b, cm_conv_w, cm_conv_b, cm_ln_w, cm_ln_b, w_out, mix_post_g, ffn2_pre_g, ffn2_w_gate, ffn2_w_up, ffn2_w_down, ffn2_post_g):
    b, s, d = x.shape
    t = b * s
    xf = x.reshape(t, d)
    for l in range(ffn1_pre_g.shape[0]):
        xf = _ffn(xf, l, ffn1_pre_g, ffn1_w_gate, ffn1_w_up, ffn1_w_down, ffn1_post_g)
        ya, yb, pc, yd = _mix_in(xf, s, l, mix_pre_g[l], w_in, sc_conv_w[l], sg_ln_w[l], sg_ln_b[l],
                                 sg_w[l], sg_b[l], cm_conv_w[l], cm_conv_b[l], cm_ln_w[l], cm_ln_b[l])
        yc = _mix_c(pc.reshape(b, s, C_COLS), rk_mu[l], rk_w0[l], rk_w_up[l], rk_a0[l], rk_a_up[l],
                    rk_g_up[l], rk_k_k[l], rk_k_a[l], rk_r_k[l].reshape(-1), rk_ln_w[l], rk_ln_b[l])
        xf = _ffn(xf, l, ffn2_pre_g, ffn2_w_gate, ffn2_w_up, ffn2_w_down, ffn2_post_g,
                  mix=((ya, yb, yc.reshape(t, GROUP_W), yd), w_out, mix_post_g))
    return xf.reshape(b, s, d)
```

```python
import functools
import math

import jax
import jax.numpy as jnp
from jax import lax
from jax.experimental import pallas as pl
from jax.experimental.pallas import tpu as pltpu

F32 = jnp.float32
BF16 = jnp.bfloat16

D_MODEL = 1024
D_FF = 2816
GROUP_W = 256
HEAD_DIM = 64
N_HEADS = 4
SC_WIDTH = 3
CHUNK = 128
CM_WIDTH = 31
RANK_W = 32
RANK_A = 32
RANK_G = 64
A_COLS = 3 * GROUP_W
B_COLS = 2 * GROUP_W
C_COLS = 3 * GROUP_W + RANK_W + RANK_A + RANK_G
D_COLS = 2 * GROUP_W
IN_COLS = A_COLS + B_COLS + C_COLS + D_COLS
LORA_W = RANK_W + RANK_A + RANK_G
RMS_EPS = 1e-6
LN_EPS = 1e-5
GN_EPS = 1e-5 * HEAD_DIM
FFN_RESID = 0.5

V7X_VMEM_BYTES = 64 * 1024 * 1024
VMEM_LIMIT_BYTES = V7X_VMEM_BYTES - 4 * 1024 * 1024
SUBLANES = 8

FFN_TILE = 1024
SEQ_TILE = 1024
FFN_PARTS = 2
FFN_FINISH_ROWS = 64
FFN_CHUNK = 256
N_FFN_CHUNKS = D_FF // FFN_CHUNK
RWKV_CHUNK = 64
RWKV_BLOCK = 4
CONV_PAD = 32
CONV_TILE = 128
PAIR_W = 2 * HEAD_DIM
assert CONV_TILE == CHUNK and CONV_PAD >= CM_WIDTH - 1 and CONV_PAD % SUBLANES == 0


def _params(*sem):
    return pltpu.CompilerParams(dimension_semantics=sem, vmem_limit_bytes=VMEM_LIMIT_BYTES)


def _resident(shape):
    nd = len(shape)
    return pl.BlockSpec(shape, lambda *_: (0,) * nd, pipeline_mode=pl.Buffered(1))


def _mm(a, b):
    return jnp.dot(a.astype(BF16), b.astype(BF16), preferred_element_type=F32)


def _mm_nt(a, b):
    return lax.dot_general(a.astype(BF16), b.astype(BF16), (((1,), (1,)), ((), ())),
                           preferred_element_type=F32)


def _mm_tn(a, b):
    return lax.dot_general(a.astype(BF16), b.astype(BF16), (((0,), (0,)), ((), ())),
                           preferred_element_type=F32)


def _interleave(*streams):
    live = list(streams)
    while live:
        live = [s for s in live if next(s, live) is not live]


def _sigmoid(x):
    return 0.5 * jnp.tanh(0.5 * x) + 0.5


def _rms(x, g):
    return x * lax.rsqrt(jnp.mean(x * x, axis=-1, keepdims=True) + RMS_EPS) * g


def _layer_norm(x, g, b, eps):
    mu = jnp.mean(x, axis=-1, keepdims=True)
    xc = x - mu
    var = jnp.mean(xc * xc, axis=-1, keepdims=True)
    return xc * lax.rsqrt(var + eps) * g + b


def _ffn_tail(x_fn, pre_g_ref, wg_ref, wu_ref, wd_ref, post_g_ref, o_ref, acc_ref, res_ref):
    n_parts = FFN_PARTS if o_ref.shape[0] == FFN_TILE else 1
    rows = o_ref.shape[0] // n_parts
    part_rows = lambda q: slice(q * rows, (q + 1) * rows)
    hidden = [None] * n_parts

    def head(q):
        x = x_fn(part_rows(q))
        res_ref[part_rows(q), :] = x
        hidden[q] = _rms(x, pre_g_ref[...]).astype(BF16)
        yield

    def swiglu(q):
        h = hidden[q]
        for c in range(N_FFN_CHUNKS):
            sl = slice(c * FFN_CHUNK, (c + 1) * FFN_CHUNK)
            g = jnp.dot(h, wg_ref[:, sl], preferred_element_type=F32)
            u = jnp.dot(h, wu_ref[:, sl], preferred_element_type=F32)
            act = (g * _sigmoid(g) * u).astype(BF16)
            part = jnp.dot(act, wd_ref[sl, :], preferred_element_type=F32)
            if c == 0:
                acc_ref[part_rows(q), :] = part
            else:
                acc_ref[part_rows(q), :] += part
            yield

    def finish(q):
        for r0 in range(q * rows, (q + 1) * rows, FFN_FINISH_ROWS):
            sl = slice(r0, r0 + FFN_FINISH_ROWS)
            o_ref[sl, :] = res_ref[sl, :] + FFN_RESID * _rms(acc_ref[sl, :], post_g_ref[...])
            yield

    _interleave(head(0))
    for q in range(n_parts):
        streams = [swiglu(q)]
        if q + 1 < n_parts:
            streams.append(head(q + 1))
        if q > 0:
            streams.append(finish(q - 1))
        _interleave(*streams)
    _interleave(finish(n_parts - 1))


def _ffn_phases(step, x_fn, pre_g_ref, wg32_ref, wu32_ref, wd32_ref, post_g_ref, o_ref,
                wg_ref, wu_ref, wd_ref, acc_ref, res_ref):
    for c in range(N_FFN_CHUNKS):
        @pl.when(step == c)
        def _(c=c):
            sl = slice(c * FFN_CHUNK, (c + 1) * FFN_CHUNK)
            wg_ref[:, sl] = wg32_ref[0].astype(BF16)
            wu_ref[:, sl] = wu32_ref[0].astype(BF16)
            wd_ref[sl, :] = wd32_ref[0].astype(BF16)

    @pl.when(step >= N_FFN_CHUNKS)
    def _():
        _ffn_tail(x_fn, pre_g_ref.at[0], wg_ref, wu_ref, wd_ref, post_g_ref.at[0], o_ref, acc_ref, res_ref)


def _ffn_body(x_ref, *refs):
    _ffn_phases(pl.program_id(0), lambda rows: x_ref[rows, :], *refs)


def _mix_out_ffn_body(x_ref, ya_ref, yb_ref, yc_ref, yd_ref, wo32_ref, mix_g_ref, *refs):
    wo_ref, ffn_refs = refs[-6], refs[:-6] + refs[-5:]
    step = pl.program_id(0)

    @pl.when(step == 0)
    def _():
        wo_ref[...] = wo32_ref[0].astype(BF16)

    def mixed(rows):
        m = None
        for i, y_ref in enumerate((ya_ref, yb_ref, yc_ref, yd_ref)):
            part = jnp.dot(y_ref[rows, :], wo_ref[i * GROUP_W:(i + 1) * GROUP_W, :],
                           preferred_element_type=F32)
            m = part if m is None else m + part
        return x_ref[rows, :] + _rms(m, mix_g_ref[0])

    _ffn_phases(step, mixed, *ffn_refs)


def _ffn(x, layer, pre_g, wg, wu, wd, post_g, mix=None):
    t, d = x.shape
    n_layers, _, f = wg.shape
    last = N_FFN_CHUNKS - 1
    rows_t = FFN_TILE
    tile = lambda width: pl.BlockSpec((rows_t, width), lambda i: (jnp.maximum(i - N_FFN_CHUNKS, 0), 0))
    gain = pl.BlockSpec((1, 1, d), lambda i: (layer, 0, 0))
    gains = lambda g: g.reshape(n_layers, 1, d)
    ffn_specs = [gain,
                 pl.BlockSpec((1, d, FFN_CHUNK), lambda i: (layer, 0, jnp.minimum(i, last))),
                 pl.BlockSpec((1, d, FFN_CHUNK), lambda i: (layer, 0, jnp.minimum(i, last))),
                 pl.BlockSpec((1, FFN_CHUNK, d), lambda i: (layer, jnp.minimum(i, last), 0)),
                 gain]
    ffn_args = (gains(pre_g), wg, wu, wd, gains(post_g))
    ffn_scratch = [pltpu.VMEM((d, f), BF16), pltpu.VMEM((d, f), BF16), pltpu.VMEM((f, d), BF16),
                   pltpu.VMEM((rows_t, d), F32), pltpu.VMEM((rows_t, d), F32)]
    if mix is None:
        body, specs, args, scratch = _ffn_body, [tile(d)], (x,), ffn_scratch
    else:
        ys, w_out, mix_g = mix
        body = _mix_out_ffn_body
        specs = ([tile(d)] + [tile(GROUP_W)] * len(ys)
                 + [pl.BlockSpec((1, d, d), lambda i: (layer, 0, 0), pipeline_mode=pl.Buffered(1)), gain])
        args = (x, *ys, w_out, gains(mix_g))
        scratch = [pltpu.VMEM((d, d), BF16)] + ffn_scratch
    return pl.pallas_call(
        body,
        grid=(N_FFN_CHUNKS + t // rows_t,),
        in_specs=specs + ffn_specs,
        out_specs=tile(d),
        out_shape=jax.ShapeDtypeStruct((t, d), F32),
        scratch_shapes=scratch,
        compiler_params=_params("arbitrary"),
        name="ffn" if mix is None else "mix_out_ffn",
    )(*args, *ffn_args)


def _causal_conv_tile(zpad_ref, w_ref, width, t0):
    zh = zpad_ref[t0:t0 + CONV_TILE + CONV_PAD, :]
    acc = None
    for r in range(min(SUBLANES, width)):
        zr = zh if r == 0 else pltpu.roll(zh, r, 0)
        for q in range((width - 1 - r) // SUBLANES + 1):
            j = width - 1 - (SUBLANES * q + r)
            start = CONV_PAD - SUBLANES * q
            term = zr[start:start + CONV_TILE] * w_ref[j:j + 1, :]
            acc = term if acc is None else acc + term
    return acc


_OFF_B = A_COLS
_OFF_C = A_COLS + B_COLS
_OFF_D = A_COLS + B_COLS + C_COLS
IN_CHUNK = 384
N_IN_CHUNKS = IN_COLS // IN_CHUNK


def _mix_in_body(x_ref, g_ref, w32_ref, *refs, tiles_per_seq):
    w_ref = refs[-3]
    step = pl.program_id(0)
    for c in range(N_IN_CHUNKS):
        @pl.when(step == c)
        def _(c=c):
            w_ref[:, c * IN_CHUNK:(c + 1) * IN_CHUNK] = w32_ref[0].astype(BF16)

    @pl.when(step >= N_IN_CHUNKS)
    def _():
        first = (step - N_IN_CHUNKS) % tiles_per_seq == 0
        _mix_in_tile(first, x_ref, g_ref, w_ref, *refs[:-3], *refs[-2:])


def _mix_in_tile(first, x_ref, g_ref, w_ref, scw_ref, sg_lnw_ref, sg_lnb_ref, sgw_ref, sgb_ref,
                 cmw_ref, cmb_ref, cm_lnw_ref, cm_lnb_ref,
                 ya_ref, yb_ref, pc_ref, yd_ref, za_ref, zd_ref):
    g = GROUP_W
    ts = x_ref.shape[0]

    @pl.when(first)
    def _():
        za_ref[0:CONV_PAD, :] = jnp.zeros((CONV_PAD, g), F32)
        zd_ref[0:CONV_PAD, :] = jnp.zeros((CONV_PAD, g), F32)

    @pl.when(jnp.logical_not(first))
    def _():
        za_ref[0:CONV_PAD, :] = za_ref[ts:ts + CONV_PAD, :]
        zd_ref[0:CONV_PAD, :] = zd_ref[ts:ts + CONV_PAD, :]

    h = _rms(x_ref[...], g_ref[...]).astype(BF16)
    z1 = jnp.dot(h, w_ref[:, _OFF_D:_OFF_D + g], preferred_element_type=F32)
    z2 = jnp.dot(h, w_ref[:, _OFF_D + g:_OFF_D + 2 * g], preferred_element_type=F32)
    zd_ref[CONV_PAD:, :] = z1 * _sigmoid(z2)
    pa = jnp.dot(h, w_ref[:, 0:A_COLS], preferred_element_type=F32)
    za_ref[CONV_PAD:, :] = pa[:, g:2 * g] * pa[:, 2 * g:3 * g]
    pb = jnp.dot(h, w_ref[:, _OFF_B:_OFF_B + B_COLS], preferred_element_type=F32)
    pc_ref[...] = jnp.dot(h, w_ref[:, _OFF_C:_OFF_C + C_COLS], preferred_element_type=F32)

    for t0 in range(0, ts, CONV_TILE):
        rows = slice(t0, t0 + CONV_TILE)
        y = _causal_conv_tile(zd_ref, cmw_ref, CM_WIDTH, t0) + cmb_ref[...]
        y = _layer_norm(y, cm_lnw_ref[...], cm_lnb_ref[...], LN_EPS)
        yd_ref[rows, :] = (y * _sigmoid(y)).astype(BF16)
        ya_ref[rows, :] = (pa[rows, 0:g] * _causal_conv_tile(za_ref, scw_ref, SC_WIDTH, t0)).astype(BF16)

    row = lax.broadcasted_iota(jnp.int32, (CHUNK, CHUNK), 0)
    col = lax.broadcasted_iota(jnp.int32, (CHUNK, CHUNK), 1)
    w_tril = [jnp.where(row >= col, sgw_ref[hd], 0.0).astype(BF16) for hd in range(N_HEADS)]
    lane_head = lax.broadcasted_iota(jnp.int32, (CHUNK, g), 1) // HEAD_DIM
    for t0 in range(0, ts, CHUNK):
        rows = slice(t0, t0 + CHUNK)
        v = _layer_norm(pb[rows, g:2 * g], sg_lnw_ref[...], sg_lnb_ref[...], LN_EPS).astype(BF16)
        s = sgb_ref[...]
        for hd in range(N_HEADS):
            sh = jnp.dot(w_tril[hd], v, preferred_element_type=F32)
            s = s + jnp.where(lane_head == hd, sh, 0.0)
        yb_ref[rows, :] = (pb[rows, 0:g] * s).astype(BF16)


def _mix_in(x, seq, layer, g, w_in, sc_conv_w, sg_ln_w, sg_ln_b, sg_w, sg_b, cm_conv_w, cm_conv_b, cm_ln_w,
            cm_ln_b):
    t, d = x.shape
    gw = GROUP_W
    vec = lambda v: v.reshape(1, gw)
    sg_bias = jnp.repeat(sg_b.T, HEAD_DIM, axis=1)
    tile = lambda width: pl.BlockSpec((SEQ_TILE, width), lambda i: (jnp.maximum(i - N_IN_CHUNKS, 0), 0))
    w_chunk = pl.BlockSpec((1, d, IN_CHUNK), lambda i: (layer, 0, jnp.minimum(i, N_IN_CHUNKS - 1)))
    return pl.pallas_call(
        functools.partial(_mix_in_body, tiles_per_seq=seq // SEQ_TILE),
        grid=(N_IN_CHUNKS + t // SEQ_TILE,),
        in_specs=[tile(d), _resident((1, d)), w_chunk, _resident((SC_WIDTH, gw)),
                  _resident((1, gw)), _resident((1, gw)), _resident((N_HEADS, CHUNK, CHUNK)),
                  _resident((CHUNK, gw)), _resident((CM_WIDTH, gw)), _resident((1, gw)),
                  _resident((1, gw)), _resident((1, gw))],
        out_specs=[tile(gw), tile(gw), tile(C_COLS), tile(gw)],
        out_shape=[jax.ShapeDtypeStruct((t, gw), BF16), jax.ShapeDtypeStruct((t, gw), BF16),
                   jax.ShapeDtypeStruct((t, C_COLS), F32), jax.ShapeDtypeStruct((t, gw), BF16)],
        scratch_shapes=[pltpu.VMEM((d, IN_COLS), BF16), pltpu.VMEM((SEQ_TILE + CONV_PAD, gw), F32),
                        pltpu.VMEM((SEQ_TILE + CONV_PAD, gw), F32)],
        compiler_params=_params("arbitrary"),
        name="mix_in",
    )(x, g.reshape(1, d), w_in, sc_conv_w, vec(sg_ln_w), vec(sg_ln_b), sg_w, sg_bias,
      cm_conv_w, vec(cm_conv_b), vec(cm_ln_w), vec(cm_ln_b))


def _split2(x):
    hi = x.astype(BF16)
    lo = (x - hi.astype(F32)).astype(BF16)
    return hi, lo


def _head_sum(x, ones_bd):
    hi, lo = _split2(x)
    return (jnp.dot(hi, ones_bd, preferred_element_type=F32)
            + jnp.dot(lo, ones_bd, preferred_element_type=F32))


def _rwkv_body(pc_ref, mu_ref, w0_ref, lora_ref, a0_ref, kk_ref, ka_ref, rk_ref, lnw_ref, lnb_ref,
               o_ref, gm_ref, yc_ref, rec_ref, bonus_ref, gate_ref, state_ref, prev_ref):
    g = GROUP_W
    lc = RWKV_CHUNK
    tb = RWKV_BLOCK * lc
    seq = pc_ref.shape[1]

    row = lax.broadcasted_iota(jnp.int32, (lc, 2 * lc), 0)
    col = lax.broadcasted_iota(jnp.int32, (lc, 2 * lc), 1) % lc
    strict2 = row > col
    incl2 = row >= col
    brow = lax.broadcasted_iota(jnp.int32, (tb, tb), 0)
    bcol = lax.broadcasted_iota(jnp.int32, (tb, tb), 1)
    tri_ones = jnp.where((brow // lc == bcol // lc) & (brow >= bcol), 1.0, 0.0).astype(BF16)
    bd_r = lax.broadcasted_iota(jnp.int32, (g, g), 0) // HEAD_DIM
    bd_c = lax.broadcasted_iota(jnp.int32, (g, g), 1) // HEAD_DIM
    ones_bd = jnp.where(bd_r == bd_c, 1.0, 0.0).astype(BF16)
    pr = lax.broadcasted_iota(jnp.int32, (PAIR_W, PAIR_W), 0)
    pcol = lax.broadcasted_iota(jnp.int32, (PAIR_W, PAIR_W), 1)
    pair_bd = (pr // HEAD_DIM) == (pcol // HEAD_DIM)
    pair_eye = pr == pcol
    first_head = lax.broadcasted_iota(jnp.int32, (lc, PAIR_W), 1) < HEAD_DIM
    first_head2 = jnp.concatenate([first_head, first_head], axis=1)
    row0 = lax.broadcasted_iota(jnp.int32, (tb, C_COLS), 0) == 0

    def split(y):
        yb = y.astype(BF16)
        m = first_head if y.shape[1] == PAIR_W else first_head2
        zero = jnp.zeros_like(yb)
        return jnp.concatenate([jnp.where(m, yb, zero), jnp.where(m, zero, yb)], axis=0)

    state_ref[...] = jnp.zeros(state_ref.shape, F32)
    prev_ref[...] = jnp.zeros(prev_ref.shape, F32)

    def build(i):
        t0 = pl.multiple_of(i * tb, tb)
        x = pc_ref[0, pl.ds(t0, tb), :]
        xs = jnp.where(row0, prev_ref[...], pltpu.roll(x, 1, 0))
        prev_ref[...] = x[tb - 1:tb, :]
        xm = x + (xs - x) * mu_ref[...]
        r = xm[:, 0:g]
        k = xm[:, g:2 * g]
        v = xm[:, 2 * g:3 * g]
        lo_in = xm[:, 3 * g:3 * g + LORA_W]
        lane = lax.broadcasted_iota(jnp.int32, (tb, LORA_W), 1)
        lo_act = jnp.where(lane < RANK_W, jnp.tanh(lo_in),
                           jnp.where(lane < RANK_W + RANK_A, lo_in, _sigmoid(lo_in)))
        lora = _mm(lo_act, lora_ref[...])
        e = _sigmoid(w0_ref[...] + lora[:, 0:g]) * math.exp(-0.5)
        a_sig = _sigmoid(a0_ref[...] + lora[:, g:2 * g])
        gate_ref[pl.ds(t0, tb), :] = lora[:, 2 * g:3 * g]
        kk = k * kk_ref[...]
        kk = kk * jnp.minimum(lax.rsqrt(_head_sum(kk * kk, ones_bd)), 1e12)
        k = k * (1.0 + (a_sig - 1.0) * ka_ref[...])
        a_ = -kk
        b_ = kk * a_sig
        bonus_ref[pl.ds(t0, tb), :] = _head_sum(r * k * rk_ref[...], ones_bd) * v

        e_hi = e.astype(BF16)
        e_r = e - e_hi.astype(F32)
        e_mid = e_r.astype(BF16)
        e_lo = (e_r - e_mid.astype(F32)).astype(BF16)
        cs = (jnp.dot(tri_ones, e_hi, preferred_element_type=F32)
              + jnp.dot(tri_ones, e_mid, preferred_element_type=F32)
              + jnp.dot(tri_ones, e_lo, preferred_element_type=F32))
        cs_last = jnp.concatenate(
            [jnp.broadcast_to(cs[(ci + 1) * lc - 1:(ci + 1) * lc, :], (lc, g)) for ci in range(RWKV_BLOCK)],
            axis=0)
        w_inc = jnp.exp(-cs)
        w_exc = jnp.exp(e - cs)
        w_inv = jnp.exp(cs)
        w_fin = jnp.exp(cs - cs_last)
        w_chunk_all = jnp.exp(-cs_last)
        at_all = a_ * w_exc
        rt_all = r * w_inc
        bt_all = b_ * w_inv
        kt_all = k * w_inv
        bh_all = b_ * w_fin
        kh_all = k * w_fin
        yield

        probs = [(ci, p) for ci in range(RWKV_BLOCK) for p in range(N_HEADS // 2)]
        ops = []
        for ci, p in probs:
            rs = slice(ci * lc, (ci + 1) * lc)
            ps = slice(p * PAIR_W, (p + 1) * PAIR_W)
            ops.append(tuple(t[rs, ps] for t in (at_all, rt_all, bt_all, kt_all, bh_all, kh_all, v)))
        a_ab, a_ak, a_rb, a_rk = [], [], [], []
        for at, rt, bt, kt, bh, kh, vp in ops:
            lhs = jnp.concatenate([at, rt], axis=0)
            ab = _mm_nt(lhs, split(bt))
            ak = _mm_nt(lhs, split(kt))
            a_ab.append(jnp.where(strict2, ab[0:lc], 0.0))
            a_ak.append(jnp.where(strict2, ak[0:lc], 0.0))
            a_rb.append(jnp.where(incl2, ab[lc:], 0.0))
            a_rk.append(jnp.where(incl2, ak[lc:], 0.0))
        yield
        zs = [jnp.concatenate([op[0], _mm(a, split(op[6]))], axis=1) for a, op in zip(a_ak, ops)]
        yield
        aps = list(a_ab)
        n_fac = int(math.log2(lc))
        for f in range(n_fac):
            zs = [z + _mm(ap, split(z)) for ap, z in zip(aps, zs)]
            if f + 1 < n_fac:
                aps = [_mm(ap, split(ap)) for ap in aps]
            yield
        for n, ((ci, p), (at, rt, bt, kt, bh, kh, vp)) in enumerate(zip(probs, ops)):
            ps = slice(p * PAIR_W, (p + 1) * PAIR_W)
            idx = i * RWKV_BLOCK + ci
            z = zs[n]
            gy = _mm(a_rb[n], split(z))
            g_mat = rt + gy[:, 0:PAIR_W]
            y0 = gy[:, PAIR_W:] + _mm(a_rk[n], split(vp))
            mc = _mm_tn(bh, z)
            kv = _mm_tn(kh, vp)
            w_chunk = w_chunk_all[ci * lc:ci * lc + 1, ps]
            m_mat = jnp.where(pair_bd, mc[:, 0:PAIR_W], 0.0) + jnp.where(pair_eye, w_chunk, 0.0)
            c_mat = jnp.where(pair_bd, mc[:, PAIR_W:] + kv, 0.0)
            gm_ref[idx, p, 0:lc, :] = g_mat.astype(BF16)
            gm_ref[idx, p, lc:, :] = m_mat.astype(BF16)
            yc_ref[idx, p, 0:lc, :] = y0
            yc_ref[idx, p, lc:, :] = c_mat

    def advance(i):
        for ci in range(RWKV_BLOCK):
            c = i * RWKV_BLOCK + ci
            t0 = pl.multiple_of(c * lc, lc)
            for p in range(N_HEADS // 2):
                step = jnp.dot(gm_ref[c, p], state_ref[p].astype(BF16), preferred_element_type=F32)
                step = step + yc_ref[c, p]
                rec_ref[pl.ds(t0, lc), p * PAIR_W:(p + 1) * PAIR_W] = step[0:lc]
                state_ref[p] = step[lc:]
            yield
        t0 = pl.multiple_of(i * tb, tb)
        o = rec_ref[pl.ds(t0, tb), :]
        mean = _head_sum(o, ones_bd) * (1.0 / HEAD_DIM)
        yield
        oc = o - mean
        var = _head_sum(oc * oc, ones_bd) * (1.0 / HEAD_DIM)
        yield
        o = oc * lax.rsqrt(var + GN_EPS) * lnw_ref[...] + lnb_ref[...]
        o = (o + bonus_ref[pl.ds(t0, tb), :]) * gate_ref[pl.ds(t0, tb), :]
        o_ref[0, pl.ds(t0, tb), :] = o.astype(BF16)

    interleave = _interleave

    n_blocks = seq // tb
    interleave(build(0))

    def body(i, carry):
        interleave(build(i), advance(i - 1))
        return carry

    lax.fori_loop(1, n_blocks, body, 0)
    interleave(advance(n_blocks - 1))


def _mix_c(pc, mu, w0, w_up, a0, a_up, g_up, k_k, k_a, r_k, ln_w, ln_b):
    b, s, _ = pc.shape
    g = GROUP_W
    n_pairs = N_HEADS // 2
    lora = jnp.zeros((LORA_W, 3 * g), F32)
    lora = lora.at[0:RANK_W, 0:g].set(w_up)
    lora = lora.at[RANK_W:RANK_W + RANK_A, g:2 * g].set(a_up)
    lora = lora.at[RANK_W + RANK_A:, 2 * g:].set(g_up)
    vec = lambda t: t.reshape(1, g)
    return pl.pallas_call(
        _rwkv_body,
        grid=(b,),
        in_specs=[pl.BlockSpec((1, s, C_COLS), lambda i: (i, 0, 0)), _resident((1, C_COLS)),
                  _resident((1, g)), _resident((LORA_W, 3 * g)), _resident((1, g)), _resident((1, g)),
                  _resident((1, g)), _resident((1, g)), _resident((1, g)), _resident((1, g))],
        out_specs=pl.BlockSpec((1, s, g), lambda i: (i, 0, 0)),
        out_shape=jax.ShapeDtypeStruct((b, s, g), BF16),
        scratch_shapes=[
            pltpu.VMEM((s // RWKV_CHUNK, n_pairs, RWKV_CHUNK + PAIR_W, PAIR_W), BF16),
            pltpu.VMEM((s // RWKV_CHUNK, n_pairs, RWKV_CHUNK + PAIR_W, PAIR_W), F32),
            pltpu.VMEM((s, g), F32),
            pltpu.VMEM((s, g), F32),
            pltpu.VMEM((s, g), F32),
            pltpu.VMEM((n_pairs, PAIR_W, PAIR_W), F32),
            pltpu.VMEM((1, C_COLS), F32),
        ],
        compiler_params=_params("parallel"),
        name="mix_c",
    )(pc, mu.reshape(1, C_COLS), vec(w0), lora.astype(BF16), vec(a0), vec(k_k), vec(k_a), vec(r_k),
      vec(ln_w), vec(ln_b))


def kernel(x, ffn1_pre_g, ffn1_w_gate, ffn1_w_up, ffn1_w_down, ffn1_post_g, mix_pre_g, w_in, sc_conv_w, sg_ln_w, sg_ln_b, sg_w, sg_b, rk_mu, rk_w0, rk_w_up, rk_a0, rk_a_up, rk_g_up, rk_k_k, rk_k_a, rk_r_k, rk_ln_w, rk_ln_b, cm_conv_w, cm_conv_b, cm_ln_w, cm_ln_b, w_out, mix_post_g, ffn2_pre_g, ffn2_w_gate, ffn2_w_up, ffn2_w_down, ffn2_post_g):
    b, s, d = x.shape
    t = b * s
    xf = x.reshape(t, d)
    for l in range(ffn1_pre_g.shape[0]):
        xf = _ffn(xf, l, ffn1_pre_g, ffn1_w_gate, ffn1_w_up, ffn1_w_down, ffn1_post_g)
        ya, yb, pc, yd = _mix_in(xf, s, l, mix_pre_g[l], w_in, sc_conv_w[l], sg_ln_w[l], sg_ln_b[l],
                                 sg_w[l], sg_b[l], cm_conv_w[l], cm_conv_b[l], cm_ln_w[l], cm_ln_b[l])
        yc = _mix_c(pc.reshape(b, s, C_COLS), rk_mu[l], rk_w0[l], rk_w_up[l], rk_a0[l], rk_a_up[l],
                    rk_g_up[l], rk_k_k[l], rk_k_a[l], rk_r_k[l].reshape(-1), rk_ln_w[l], rk_ln_b[l])
        xf = _ffn(xf, l, ffn2_pre_g, ffn2_w_gate, ffn2_w_up, ffn2_w_down, ffn2_post_g,
                  mix=((ya, yb, yc.reshape(t, GROUP_W), yd), w_out, mix_post_g))
    return xf.reshape(b, s, d)
```

```python
import functools
import math

import jax
import jax.numpy as jnp
from jax import lax
from jax.experimental import pallas as pl
from jax.experimental.pallas import tpu as pltpu

F32 = jnp.float32
BF16 = jnp.bfloat16

D_MODEL = 1024
D_FF = 2816
GROUP_W = 256
HEAD_DIM = 64
N_HEADS = 4
SC_WIDTH = 3
CHUNK = 128
CM_WIDTH = 31
RANK_W = 32
RANK_A = 32
RANK_G = 64
A_COLS = 3 * GROUP_W
B_COLS = 2 * GROUP_W
C_COLS = 3 * GROUP_W + RANK_W + RANK_A + RANK_G
D_COLS = 2 * GROUP_W
IN_COLS = A_COLS + B_COLS + C_COLS + D_COLS
LORA_W = RANK_W + RANK_A + RANK_G
RMS_EPS = 1e-6
LN_EPS = 1e-5
GN_EPS = 1e-5 * HEAD_DIM
FFN_RESID = 0.5

V7X_VMEM_BYTES = 64 * 1024 * 1024
VMEM_LIMIT_BYTES = V7X_VMEM_BYTES - 8 * 1024 * 1024
SUBLANES = 8

FFN_TILE = 512
SEQ_TILE = 1024
FFN_CHUNK = 256
N_FFN_CHUNKS = D_FF // FFN_CHUNK
RWKV_CHUNK = 64
RWKV_BLOCK = 4
CONV_PAD = 32
CONV_TILE = 128
PAIR_W = 2 * HEAD_DIM
assert CONV_PAD >= CM_WIDTH - 1 and CONV_PAD % SUBLANES == 0


def _params(*sem):
    return pltpu.CompilerParams(dimension_semantics=sem, vmem_limit_bytes=VMEM_LIMIT_BYTES)


def _resident(shape):
    nd = len(shape)
    return pl.BlockSpec(shape, lambda *_: (0,) * nd, pipeline_mode=pl.Buffered(1))


def _mm(a, b):
    return jnp.dot(a.astype(BF16), b.astype(BF16), preferred_element_type=F32)


def _mm_nt(a, b):
    return lax.dot_general(a.astype(BF16), b.astype(BF16), (((1,), (1,)), ((), ())),
                           preferred_element_type=F32)


def _mm_tn(a, b):
    return lax.dot_general(a.astype(BF16), b.astype(BF16), (((0,), (0,)), ((), ())),
                           preferred_element_type=F32)


def _interleave(*streams):
    live = list(streams)
    while live:
        live = [s for s in live if next(s, live) is not live]


def _sigmoid(x):
    return 0.5 * jnp.tanh(0.5 * x) + 0.5


def _rms(x, g):
    return x * lax.rsqrt(jnp.mean(x * x, axis=-1, keepdims=True) + RMS_EPS) * g


def _layer_norm(x, g, b, eps):
    mu = jnp.mean(x, axis=-1, keepdims=True)
    xc = x - mu
    var = jnp.mean(xc * xc, axis=-1, keepdims=True)
    return xc * lax.rsqrt(var + eps) * g + b


def _ffn_tail(x, pre_g_ref, wg_ref, wu_ref, wd_ref, post_g_ref, o_ref, acc_ref):
    h = _rms(x, pre_g_ref[...]).astype(BF16)
    for c in range(D_FF // FFN_CHUNK):
        sl = slice(c * FFN_CHUNK, (c + 1) * FFN_CHUNK)
        g = jnp.dot(h, wg_ref[:, sl], preferred_element_type=F32)
        u = jnp.dot(h, wu_ref[:, sl], preferred_element_type=F32)
        act = (g * _sigmoid(g) * u).astype(BF16)
        part = jnp.dot(act, wd_ref[sl, :], preferred_element_type=F32)
        if c == 0:
            acc_ref[...] = part
        else:
            acc_ref[...] += part
    o_ref[...] = x + FFN_RESID * _rms(acc_ref[...], post_g_ref[...])


def _ffn_phases(step, x_fn, pre_g_ref, wg32_ref, wu32_ref, wd32_ref, post_g_ref, o_ref,
                wg_ref, wu_ref, wd_ref, acc_ref):
    for c in range(N_FFN_CHUNKS):
        @pl.when(step == c)
        def _(c=c):
            sl = slice(c * FFN_CHUNK, (c + 1) * FFN_CHUNK)
            wg_ref[:, sl] = wg32_ref[0].astype(BF16)
            wu_ref[:, sl] = wu32_ref[0].astype(BF16)
            wd_ref[sl, :] = wd32_ref[0].astype(BF16)

    @pl.when(step >= N_FFN_CHUNKS)
    def _():
        _ffn_tail(x_fn(), pre_g_ref.at[0], wg_ref, wu_ref, wd_ref, post_g_ref.at[0], o_ref, acc_ref)


def _ffn_body(x_ref, *refs):
    _ffn_phases(pl.program_id(0), lambda: x_ref[...], *refs)


def _mix_out_ffn_body(x_ref, ya_ref, yb_ref, yc_ref, yd_ref, wo32_ref, mix_g_ref, *refs):
    wo_ref, ffn_refs = refs[-5], refs[:-5] + refs[-4:]
    step = pl.program_id(0)

    @pl.when(step == 0)
    def _():
        wo_ref[...] = wo32_ref[0].astype(BF16)

    def mixed():
        m = None
        for i, y_ref in enumerate((ya_ref, yb_ref, yc_ref, yd_ref)):
            part = jnp.dot(y_ref[...], wo_ref[i * GROUP_W:(i + 1) * GROUP_W, :], preferred_element_type=F32)
            m = part if m is None else m + part
        return x_ref[...] + _rms(m, mix_g_ref[0])

    _ffn_phases(step, mixed, *ffn_refs)


def _ffn(x, layer, pre_g, wg, wu, wd, post_g, mix=None):
    t, d = x.shape
    n_layers, _, f = wg.shape
    last = N_FFN_CHUNKS - 1
    tile = lambda width: pl.BlockSpec((FFN_TILE, width), lambda i: (jnp.maximum(i - N_FFN_CHUNKS, 0), 0))
    gain = pl.BlockSpec((1, 1, d), lambda i: (layer, 0, 0))
    gains = lambda g: g.reshape(n_layers, 1, d)
    ffn_specs = [gain,
                 pl.BlockSpec((1, d, FFN_CHUNK), lambda i: (layer, 0, jnp.minimum(i, last))),
                 pl.BlockSpec((1, d, FFN_CHUNK), lambda i: (layer, 0, jnp.minimum(i, last))),
                 pl.BlockSpec((1, FFN_CHUNK, d), lambda i: (layer, jnp.minimum(i, last), 0)),
                 gain]
    ffn_args = (gains(pre_g), wg, wu, wd, gains(post_g))
    ffn_scratch = [pltpu.VMEM((d, f), BF16), pltpu.VMEM((d, f), BF16), pltpu.VMEM((f, d), BF16),
                   pltpu.VMEM((FFN_TILE, d), F32)]
    if mix is None:
        body, specs, args, scratch = _ffn_body, [tile(d)], (x,), ffn_scratch
    else:
        ys, w_out, mix_g = mix
        body = _mix_out_ffn_body
        specs = ([tile(d)] + [tile(GROUP_W)] * len(ys)
                 + [pl.BlockSpec((1, d, d), lambda i: (layer, 0, 0), pipeline_mode=pl.Buffered(1)), gain])
        args = (x, *ys, w_out, gains(mix_g))
        scratch = [pltpu.VMEM((d, d), BF16)] + ffn_scratch
    return pl.pallas_call(
        body,
        grid=(N_FFN_CHUNKS + t // FFN_TILE,),
        in_specs=specs + ffn_specs,
        out_specs=tile(d),
        out_shape=jax.ShapeDtypeStruct((t, d), F32),
        scratch_shapes=scratch,
        compiler_params=_params("arbitrary"),
        name="ffn" if mix is None else "mix_out_ffn",
    )(*args, *ffn_args)


def _causal_conv_tile(zpad_ref, w_ref, width, t0):
    zh = zpad_ref[t0:t0 + CONV_TILE + CONV_PAD, :]
    acc = None
    for r in range(min(SUBLANES, width)):
        zr = zh if r == 0 else pltpu.roll(zh, r, 0)
        for q in range((width - 1 - r) // SUBLANES + 1):
            j = width - 1 - (SUBLANES * q + r)
            start = CONV_PAD - SUBLANES * q
            term = zr[start:start + CONV_TILE] * w_ref[j:j + 1, :]
            acc = term if acc is None else acc + term
    return acc


_OFF_B = A_COLS
_OFF_C = A_COLS + B_COLS
_OFF_D = A_COLS + B_COLS + C_COLS
IN_CHUNK = 384
N_IN_CHUNKS = IN_COLS // IN_CHUNK


def _mix_in_body(x_ref, g_ref, w32_ref, *refs, tiles_per_seq):
    w_ref = refs[-3]
    step = pl.program_id(0)
    for c in range(N_IN_CHUNKS):
        @pl.when(step == c)
        def _(c=c):
            w_ref[:, c * IN_CHUNK:(c + 1) * IN_CHUNK] = w32_ref[0].astype(BF16)

    @pl.when(step >= N_IN_CHUNKS)
    def _():
        first = (step - N_IN_CHUNKS) % tiles_per_seq == 0
        _mix_in_tile(first, x_ref, g_ref, w_ref, *refs[:-3], *refs[-2:])


def _mix_in_tile(first, x_ref, g_ref, w_ref, scw_ref, sg_lnw_ref, sg_lnb_ref, sgw_ref, sgb_ref,
                 cmw_ref, cmb_ref, cm_lnw_ref, cm_lnb_ref,
                 ya_ref, yb_ref, pc_ref, yd_ref, za_ref, zd_ref):
    g = GROUP_W
    ts = x_ref.shape[0]

    @pl.when(first)
    def _():
        za_ref[0:CONV_PAD, :] = jnp.zeros((CONV_PAD, g), F32)
        zd_ref[0:CONV_PAD, :] = jnp.zeros((CONV_PAD, g), F32)

    @pl.when(jnp.logical_not(first))
    def _():
        za_ref[0:CONV_PAD, :] = za_ref[ts:ts + CONV_PAD, :]
        zd_ref[0:CONV_PAD, :] = zd_ref[ts:ts + CONV_PAD, :]

    h = _rms(x_ref[...], g_ref[...]).astype(BF16)
    z1 = jnp.dot(h, w_ref[:, _OFF_D:_OFF_D + g], preferred_element_type=F32)
    z2 = jnp.dot(h, w_ref[:, _OFF_D + g:_OFF_D + 2 * g], preferred_element_type=F32)
    zd_ref[CONV_PAD:, :] = z1 * _sigmoid(z2)
    pa = jnp.dot(h, w_ref[:, 0:A_COLS], preferred_element_type=F32)
    za_ref[CONV_PAD:, :] = pa[:, g:2 * g] * pa[:, 2 * g:3 * g]
    pb = jnp.dot(h, w_ref[:, _OFF_B:_OFF_B + B_COLS], preferred_element_type=F32)
    pc_ref[...] = jnp.dot(h, w_ref[:, _OFF_C:_OFF_C + C_COLS], preferred_element_type=F32)

    for t0 in range(0, ts, CONV_TILE):
        rows = slice(t0, t0 + CONV_TILE)
        y = _causal_conv_tile(zd_ref, cmw_ref, CM_WIDTH, t0) + cmb_ref[...]
        y = _layer_norm(y, cm_lnw_ref[...], cm_lnb_ref[...], LN_EPS)
        yd_ref[rows, :] = (y * _sigmoid(y)).astype(BF16)
        ya_ref[rows, :] = (pa[rows, 0:g] * _causal_conv_tile(za_ref, scw_ref, SC_WIDTH, t0)).astype(BF16)

    row = lax.broadcasted_iota(jnp.int32, (CHUNK, CHUNK), 0)
    col = lax.broadcasted_iota(jnp.int32, (CHUNK, CHUNK), 1)
    w_tril = [jnp.where(row >= col, sgw_ref[hd], 0.0).astype(BF16) for hd in range(N_HEADS)]
    lane_head = lax.broadcasted_iota(jnp.int32, (CHUNK, g), 1) // HEAD_DIM
    for t0 in range(0, ts, CHUNK):
        rows = slice(t0, t0 + CHUNK)
        v = _layer_norm(pb[rows, g:2 * g], sg_lnw_ref[...], sg_lnb_ref[...], LN_EPS).astype(BF16)
        s = sgb_ref[...]
        for hd in range(N_HEADS):
            sh = jnp.dot(w_tril[hd], v, preferred_element_type=F32)
            s = s + jnp.where(lane_head == hd, sh, 0.0)
        yb_ref[rows, :] = (pb[rows, 0:g] * s).astype(BF16)


def _mix_in(x, seq, layer, g, w_in, sc_conv_w, sg_ln_w, sg_ln_b, sg_w, sg_b, cm_conv_w, cm_conv_b, cm_ln_w,
            cm_ln_b):
    t, d = x.shape
    gw = GROUP_W
    vec = lambda v: v.reshape(1, gw)
    sg_bias = jnp.repeat(sg_b.T, HEAD_DIM, axis=1)
    tile = lambda width: pl.BlockSpec((SEQ_TILE, width), lambda i: (jnp.maximum(i - N_IN_CHUNKS, 0), 0))
    w_chunk = pl.BlockSpec((1, d, IN_CHUNK), lambda i: (layer, 0, jnp.minimum(i, N_IN_CHUNKS - 1)))
    return pl.pallas_call(
        functools.partial(_mix_in_body, tiles_per_seq=seq // SEQ_TILE),
        grid=(N_IN_CHUNKS + t // SEQ_TILE,),
        in_specs=[tile(d), _resident((1, d)), w_chunk, _resident((SC_WIDTH, gw)),
                  _resident((1, gw)), _resident((1, gw)), _resident((N_HEADS, CHUNK, CHUNK)),
                  _resident((CHUNK, gw)), _resident((CM_WIDTH, gw)), _resident((1, gw)),
                  _resident((1, gw)), _resident((1, gw))],
        out_specs=[tile(gw), tile(gw), tile(C_COLS), tile(gw)],
        out_shape=[jax.ShapeDtypeStruct((t, gw), BF16), jax.ShapeDtypeStruct((t, gw), BF16),
                   jax.ShapeDtypeStruct((t, C_COLS), F32), jax.ShapeDtypeStruct((t, gw), BF16)],
        scratch_shapes=[pltpu.VMEM((d, IN_COLS), BF16), pltpu.VMEM((SEQ_TILE + CONV_PAD, gw), F32),
                        pltpu.VMEM((SEQ_TILE + CONV_PAD, gw), F32)],
        compiler_params=_params("arbitrary"),
        name="mix_in",
    )(x, g.reshape(1, d), w_in, sc_conv_w, vec(sg_ln_w), vec(sg_ln_b), sg_w, sg_bias,
      cm_conv_w, vec(cm_conv_b), vec(cm_ln_w), vec(cm_ln_b))


def _split2(x):
    hi = x.astype(BF16)
    lo = (x - hi.astype(F32)).astype(BF16)
    return hi, lo


def _head_sum(x, ones_bd):
    hi, lo = _split2(x)
    return (jnp.dot(hi, ones_bd, preferred_element_type=F32)
            + jnp.dot(lo, ones_bd, preferred_element_type=F32))


def _rwkv_body(pc_ref, mu_ref, w0_ref, lora_ref, a0_ref, kk_ref, ka_ref, rk_ref, lnw_ref, lnb_ref,
               o_ref, gm_ref, yc_ref, rec_ref, bonus_ref, gate_ref, state_ref, prev_ref):
    g = GROUP_W
    lc = RWKV_CHUNK
    tb = RWKV_BLOCK * lc
    seq = pc_ref.shape[1]

    row = lax.broadcasted_iota(jnp.int32, (lc, 2 * lc), 0)
    col = lax.broadcasted_iota(jnp.int32, (lc, 2 * lc), 1) % lc
    strict2 = row > col
    incl2 = row >= col
    brow = lax.broadcasted_iota(jnp.int32, (tb, tb), 0)
    bcol = lax.broadcasted_iota(jnp.int32, (tb, tb), 1)
    tri_ones = jnp.where((brow // lc == bcol // lc) & (brow >= bcol), 1.0, 0.0).astype(BF16)
    bd_r = lax.broadcasted_iota(jnp.int32, (g, g), 0) // HEAD_DIM
    bd_c = lax.broadcasted_iota(jnp.int32, (g, g), 1) // HEAD_DIM
    ones_bd = jnp.where(bd_r == bd_c, 1.0, 0.0).astype(BF16)
    pr = lax.broadcasted_iota(jnp.int32, (PAIR_W, PAIR_W), 0)
    pcol = lax.broadcasted_iota(jnp.int32, (PAIR_W, PAIR_W), 1)
    pair_bd = (pr // HEAD_DIM) == (pcol // HEAD_DIM)
    pair_eye = pr == pcol
    first_head = lax.broadcasted_iota(jnp.int32, (lc, PAIR_W), 1) < HEAD_DIM
    first_head2 = jnp.concatenate([first_head, first_head], axis=1)
    row0 = lax.broadcasted_iota(jnp.int32, (tb, C_COLS), 0) == 0

    def split(y):
        yb = y.astype(BF16)
        m = first_head if y.shape[1] == PAIR_W else first_head2
        zero = jnp.zeros_like(yb)
        return jnp.concatenate([jnp.where(m, yb, zero), jnp.where(m, zero, yb)], axis=0)

    state_ref[...] = jnp.zeros(state_ref.shape, F32)
    prev_ref[...] = jnp.zeros(prev_ref.shape, F32)

    def build(i):
        t0 = pl.multiple_of(i * tb, tb)
        x = pc_ref[0, pl.ds(t0, tb), :]
        xs = jnp.where(row0, prev_ref[...], pltpu.roll(x, 1, 0))
        prev_ref[...] = x[tb - 1:tb, :]
        xm = x + (xs - x) * mu_ref[...]
        r = xm[:, 0:g]
        k = xm[:, g:2 * g]
        v = xm[:, 2 * g:3 * g]
        lo_in = xm[:, 3 * g:3 * g + LORA_W]
        lane = lax.broadcasted_iota(jnp.int32, (tb, LORA_W), 1)
        lo_act = jnp.where(lane < RANK_W, jnp.tanh(lo_in),
                           jnp.where(lane < RANK_W + RANK_A, lo_in, _sigmoid(lo_in)))
        lora = _mm(lo_act, lora_ref[...])
        e = _sigmoid(w0_ref[...] + lora[:, 0:g]) * math.exp(-0.5)
        a_sig = _sigmoid(a0_ref[...] + lora[:, g:2 * g])
        gate_ref[pl.ds(t0, tb), :] = lora[:, 2 * g:3 * g]
        kk = k * kk_ref[...]
        kk = kk * jnp.minimum(lax.rsqrt(_head_sum(kk * kk, ones_bd)), 1e12)
        k = k * (1.0 + (a_sig - 1.0) * ka_ref[...])
        a_ = -kk
        b_ = kk * a_sig
        bonus_ref[pl.ds(t0, tb), :] = _head_sum(r * k * rk_ref[...], ones_bd) * v

        e_hi = e.astype(BF16)
        e_r = e - e_hi.astype(F32)
        e_mid = e_r.astype(BF16)
        e_lo = (e_r - e_mid.astype(F32)).astype(BF16)
        cs = (jnp.dot(tri_ones, e_hi, preferred_element_type=F32)
              + jnp.dot(tri_ones, e_mid, preferred_element_type=F32)
              + jnp.dot(tri_ones, e_lo, preferred_element_type=F32))
        cs_last = jnp.concatenate(
            [jnp.broadcast_to(cs[(ci + 1) * lc - 1:(ci + 1) * lc, :], (lc, g)) for ci in range(RWKV_BLOCK)],
            axis=0)
        w_inc = jnp.exp(-cs)
        w_exc = jnp.exp(e - cs)
        w_inv = jnp.exp(cs)
        w_fin = jnp.exp(cs - cs_last)
        w_chunk_all = jnp.exp(-cs_last)
        at_all = a_ * w_exc
        rt_all = r * w_inc
        bt_all = b_ * w_inv
        kt_all = k * w_inv
        bh_all = b_ * w_fin
        kh_all = k * w_fin
        yield

        probs = [(ci, p) for ci in range(RWKV_BLOCK) for p in range(N_HEADS // 2)]
        ops = []
        for ci, p in probs:
            rs = slice(ci * lc, (ci + 1) * lc)
            ps = slice(p * PAIR_W, (p + 1) * PAIR_W)
            ops.append(tuple(t[rs, ps] for t in (at_all, rt_all, bt_all, kt_all, bh_all, kh_all, v)))
        a_ab, a_ak, a_rb, a_rk = [], [], [], []
        for at, rt, bt, kt, bh, kh, vp in ops:
            lhs = jnp.concatenate([at, rt], axis=0)
            ab = _mm_nt(lhs, split(bt))
            ak = _mm_nt(lhs, split(kt))
            a_ab.append(jnp.where(strict2, ab[0:lc], 0.0))
            a_ak.append(jnp.where(strict2, ak[0:lc], 0.0))
            a_rb.append(jnp.where(incl2, ab[lc:], 0.0))
            a_rk.append(jnp.where(incl2, ak[lc:], 0.0))
        yield
        zs = [jnp.concatenate([op[0], _mm(a, split(op[6]))], axis=1) for a, op in zip(a_ak, ops)]
        yield
        aps = list(a_ab)
        n_fac = int(math.log2(lc))
        for f in range(n_fac):
            zs = [z + _mm(ap, split(z)) for ap, z in zip(aps, zs)]
            if f + 1 < n_fac:
                aps = [_mm(ap, split(ap)) for ap in aps]
            yield
        for n, ((ci, p), (at, rt, bt, kt, bh, kh, vp)) in enumerate(zip(probs, ops)):
            ps = slice(p * PAIR_W, (p + 1) * PAIR_W)
            idx = i * RWKV_BLOCK + ci
            z = zs[n]
            gy = _mm(a_rb[n], split(z))
            g_mat = rt + gy[:, 0:PAIR_W]
            y0 = gy[:, PAIR_W:] + _mm(a_rk[n], split(vp))
            mc = _mm_tn(bh, z)
            kv = _mm_tn(kh, vp)
            w_chunk = w_chunk_all[ci * lc:ci * lc + 1, ps]
            m_mat = jnp.where(pair_bd, mc[:, 0:PAIR_W], 0.0) + jnp.where(pair_eye, w_chunk, 0.0)
            c_mat = jnp.where(pair_bd, mc[:, PAIR_W:] + kv, 0.0)
            gm_ref[idx, p, 0:lc, :] = g_mat.astype(BF16)
            gm_ref[idx, p, lc:, :] = m_mat.astype(BF16)
            yc_ref[idx, p, 0:lc, :] = y0
            yc_ref[idx, p, lc:, :] = c_mat

    def advance(i):
        for ci in range(RWKV_BLOCK):
            c = i * RWKV_BLOCK + ci
            t0 = pl.multiple_of(c * lc, lc)
            for p in range(N_HEADS // 2):
                step = jnp.dot(gm_ref[c, p], state_ref[p].astype(BF16), preferred_element_type=F32)
                step = step + yc_ref[c, p]
                rec_ref[pl.ds(t0, lc), p * PAIR_W:(p + 1) * PAIR_W] = step[0:lc]
                state_ref[p] = step[lc:]
            yield
        t0 = pl.multiple_of(i * tb, tb)
        o = rec_ref[pl.ds(t0, tb), :]
        mean = _head_sum(o, ones_bd) * (1.0 / HEAD_DIM)
        yield
        oc = o - mean
        var = _head_sum(oc * oc, ones_bd) * (1.0 / HEAD_DIM)
        yield
        o = oc * lax.rsqrt(var + GN_EPS) * lnw_ref[...] + lnb_ref[...]
        o = (o + bonus_ref[pl.ds(t0, tb), :]) * gate_ref[pl.ds(t0, tb), :]
        o_ref[0, pl.ds(t0, tb), :] = o.astype(BF16)

    n_blocks = seq // tb
    _interleave(build(0))

    def body(i, carry):
        _interleave(build(i), advance(i - 1))
        return carry

    lax.fori_loop(1, n_blocks, body, 0)
    _interleave(advance(n_blocks - 1))


def _mix_c(pc, mu, w0, w_up, a0, a_up, g_up, k_k, k_a, r_k, ln_w, ln_b):
    b, s, _ = pc.shape
    g = GROUP_W
    n_pairs = N_HEADS // 2
    lora = jnp.zeros((LORA_W, 3 * g), F32)
    lora = lora.at[0:RANK_W, 0:g].set(w_up)
    lora = lora.at[RANK_W:RANK_W + RANK_A, g:2 * g].set(a_up)
    lora = lora.at[RANK_W + RANK_A:, 2 * g:].set(g_up)
    vec = lambda t: t.reshape(1, g)
    return pl.pallas_call(
        _rwkv_body,
        grid=(b,),
        in_specs=[pl.BlockSpec((1, s, C_COLS), lambda i: (i, 0, 0)), _resident((1, C_COLS)),
                  _resident((1, g)), _resident((LORA_W, 3 * g)), _resident((1, g)), _resident((1, g)),
                  _resident((1, g)), _resident((1, g)), _resident((1, g)), _resident((1, g))],
        out_specs=pl.BlockSpec((1, s, g), lambda i: (i, 0, 0)),
        out_shape=jax.ShapeDtypeStruct((b, s, g), BF16),
        scratch_shapes=[
            pltpu.VMEM((s // RWKV_CHUNK, n_pairs, RWKV_CHUNK + PAIR_W, PAIR_W), BF16),
            pltpu.VMEM((s // RWKV_CHUNK, n_pairs, RWKV_CHUNK + PAIR_W, PAIR_W), F32),
            pltpu.VMEM((s, g), F32),
            pltpu.VMEM((s, g), F32),
            pltpu.VMEM((s, g), F32),
            pltpu.VMEM((n_pairs, PAIR_W, PAIR_W), F32),
            pltpu.VMEM((1, C_COLS), F32),
        ],
        compiler_params=_params("parallel"),
        name="mix_c",
    )(pc, mu.reshape(1, C_COLS), vec(w0), lora.astype(BF16), vec(a0), vec(k_k), vec(k_a), vec(r_k),
      vec(ln_w), vec(ln_b))


def kernel(x, ffn1_pre_g, ffn1_w_gate, ffn1_w_up, ffn1_w_down, ffn1_post_g, mix_pre_g, w_in, sc_conv_w, sg_ln_w, sg_ln_b, sg_w, sg_b, rk_mu, rk_w0, rk_w_up, rk_a0, rk_a_up, rk_g_up, rk_k_k, rk_k_a, rk_r_k, rk_ln_w, rk_ln_b, cm_conv_w, cm_conv_b, cm_ln_w, cm_ln_b, w_out, mix_post_g, ffn2_pre_g, ffn2_w_gate, ffn2_w_up, ffn2_w_down, ffn2_post_g):
    b, s, d = x.shape
    t = b * s
    xf = x.reshape(t, d)
    for l in range(ffn1_pre_g.shape[0]):
        xf = _ffn(xf, l, ffn1_pre_g, ffn1_w_gate, ffn1_w_up, ffn1_w_down, ffn1_post_g)
        ya, yb, pc, yd = _mix_in(xf, s, l, mix_pre_g[l], w_in, sc_conv_w[l], sg_ln_w[l], sg_ln_b[l],
                                 sg_w[l], sg_b[l], cm_conv_w[l], cm_conv_b[l], cm_ln_w[l], cm_ln_b[l])
        yc = _mix_c(pc.reshape(b, s, C_COLS), rk_mu[l], rk_w0[l], rk_w_up[l], rk_a0[l], rk_a_up[l],
                    rk_g_up[l], rk_k_k[l], rk_k_a[l], rk_r_k[l].reshape(-1), rk_ln_w[l], rk_ln_b[l])
        xf = _ffn(xf, l, ffn2_pre_g, ffn2_w_gate, ffn2_w_up, ffn2_w_down, ffn2_post_g,
                  mix=((ya, yb, yc.reshape(t, GROUP_W), yd), w_out, mix_post_g))
    return xf.reshape(b, s, d)
```

```python
import functools
import math

import jax
import jax.numpy as jnp
from jax import lax
from jax.experimental import pallas as pl
from jax.experimental.pallas import tpu as pltpu

F32 = jnp.float32
BF16 = jnp.bfloat16

D_MODEL = 1024
D_FF = 2816
GROUP_W = 256
HEAD_DIM = 64
N_HEADS = 4
SC_WIDTH = 3
CHUNK = 128
CM_WIDTH = 31
RANK_W = 32
RANK_A = 32
RANK_G = 64
A_COLS = 3 * GROUP_W
B_COLS = 2 * GROUP_W
C_COLS = 3 * GROUP_W + RANK_W + RANK_A + RANK_G
D_COLS = 2 * GROUP_W
IN_COLS = A_COLS + B_COLS + C_COLS + D_COLS
LORA_W = RANK_W + RANK_A + RANK_G
RMS_EPS = 1e-6
LN_EPS = 1e-5
GN_EPS = 1e-5 * HEAD_DIM
FFN_RESID = 0.5

V7X_VMEM_BYTES = 64 * 1024 * 1024
VMEM_LIMIT_BYTES = V7X_VMEM_BYTES - 8 * 1024 * 1024
SUBLANES = 8

FFN_TILE = 512
SEQ_TILE = 1024
FFN_CHUNK = 256
N_FFN_CHUNKS = D_FF // FFN_CHUNK
RWKV_CHUNK = 64
RWKV_BLOCK = 4
CONV_PAD = 32
CONV_TILE = 128
PAIR_W = 2 * HEAD_DIM
assert CONV_PAD >= CM_WIDTH - 1 and CONV_PAD % SUBLANES == 0


def _params(*sem):
    return pltpu.CompilerParams(dimension_semantics=sem, vmem_limit_bytes=VMEM_LIMIT_BYTES)


def _resident(shape):
    nd = len(shape)
    return pl.BlockSpec(shape, lambda *_: (0,) * nd, pipeline_mode=pl.Buffered(1))


def _mm(a, b):
    return jnp.dot(a.astype(BF16), b.astype(BF16), preferred_element_type=F32)


def _mm_nt(a, b):
    return lax.dot_general(a.astype(BF16), b.astype(BF16), (((1,), (1,)), ((), ())),
                           preferred_element_type=F32)


def _mm_tn(a, b):
    return lax.dot_general(a.astype(BF16), b.astype(BF16), (((0,), (0,)), ((), ())),
                           preferred_element_type=F32)


def _interleave(*streams):
    live = list(streams)
    while live:
        live = [s for s in live if next(s, live) is not live]


def _sigmoid(x):
    return 0.5 * jnp.tanh(0.5 * x) + 0.5


def _rms(x, g):
    return x * lax.rsqrt(jnp.mean(x * x, axis=-1, keepdims=True) + RMS_EPS) * g


def _layer_norm(x, g, b, eps):
    mu = jnp.mean(x, axis=-1, keepdims=True)
    xc = x - mu
    var = jnp.mean(xc * xc, axis=-1, keepdims=True)
    return xc * lax.rsqrt(var + eps) * g + b


def _ffn_tail(x, pre_g_ref, wg_ref, wu_ref, wd_ref, post_g_ref, o_ref, acc_ref):
    h = _rms(x, pre_g_ref[...]).astype(BF16)
    for c in range(D_FF // FFN_CHUNK):
        sl = slice(c * FFN_CHUNK, (c + 1) * FFN_CHUNK)
        g = jnp.dot(h, wg_ref[:, sl], preferred_element_type=F32)
        u = jnp.dot(h, wu_ref[:, sl], preferred_element_type=F32)
        act = (g * _sigmoid(g) * u).astype(BF16)
        part = jnp.dot(act, wd_ref[sl, :], preferred_element_type=F32)
        if c == 0:
            acc_ref[...] = part
        else:
            acc_ref[...] += part
    o_ref[...] = x + FFN_RESID * _rms(acc_ref[...], post_g_ref[...])


def _ffn_phases(step, x_fn, pre_g_ref, wg32_ref, wu32_ref, wd32_ref, post_g_ref, o_ref,
                wg_ref, wu_ref, wd_ref, acc_ref):
    for c in range(N_FFN_CHUNKS):
        @pl.when(step == c)
        def _(c=c):
            sl = slice(c * FFN_CHUNK, (c + 1) * FFN_CHUNK)
            wg_ref[:, sl] = wg32_ref[0].astype(BF16)
            wu_ref[:, sl] = wu32_ref[0].astype(BF16)
            wd_ref[sl, :] = wd32_ref[0].astype(BF16)

    @pl.when(step >= N_FFN_CHUNKS)
    def _():
        _ffn_tail(x_fn(), pre_g_ref.at[0], wg_ref, wu_ref, wd_ref, post_g_ref.at[0], o_ref, acc_ref)


def _ffn_body(x_ref, *refs):
    _ffn_phases(pl.program_id(0), lambda: x_ref[...], *refs)


def _mix_out_ffn_body(x_ref, ya_ref, yb_ref, yc_ref, yd_ref, wo32_ref, mix_g_ref, *refs):
    wo_ref, ffn_refs = refs[-5], refs[:-5] + refs[-4:]
    step = pl.program_id(0)

    @pl.when(step == 0)
    def _():
        wo_ref[...] = wo32_ref[0].astype(BF16)

    def mixed():
        m = None
        for i, y_ref in enumerate((ya_ref, yb_ref, yc_ref, yd_ref)):
            part = jnp.dot(y_ref[...], wo_ref[i * GROUP_W:(i + 1) * GROUP_W, :], preferred_element_type=F32)
            m = part if m is None else m + part
        return x_ref[...] + _rms(m, mix_g_ref[0])

    _ffn_phases(step, mixed, *ffn_refs)


def _ffn(x, layer, pre_g, wg, wu, wd, post_g, mix=None):
    t, d = x.shape
    n_layers, _, f = wg.shape
    last = N_FFN_CHUNKS - 1
    tile = lambda width: pl.BlockSpec((FFN_TILE, width), lambda i: (jnp.maximum(i - N_FFN_CHUNKS, 0), 0))
    gain = pl.BlockSpec((1, 1, d), lambda i: (layer, 0, 0))
    gains = lambda g: g.reshape(n_layers, 1, d)
    ffn_specs = [gain,
                 pl.BlockSpec((1, d, FFN_CHUNK), lambda i: (layer, 0, jnp.minimum(i, last))),
                 pl.BlockSpec((1, d, FFN_CHUNK), lambda i: (layer, 0, jnp.minimum(i, last))),
                 pl.BlockSpec((1, FFN_CHUNK, d), lambda i: (layer, jnp.minimum(i, last), 0)),
                 gain]
    ffn_args = (gains(pre_g), wg, wu, wd, gains(post_g))
    ffn_scratch = [pltpu.VMEM((d, f), BF16), pltpu.VMEM((d, f), BF16), pltpu.VMEM((f, d), BF16),
                   pltpu.VMEM((FFN_TILE, d), F32)]
    if mix is None:
        body, specs, args, scratch = _ffn_body, [tile(d)], (x,), ffn_scratch
    else:
        ys, w_out, mix_g = mix
        body = _mix_out_ffn_body
        specs = ([tile(d)] + [tile(GROUP_W)] * len(ys)
                 + [pl.BlockSpec((1, d, d), lambda i: (layer, 0, 0), pipeline_mode=pl.Buffered(1)), gain])
        args = (x, *ys, w_out, gains(mix_g))
        scratch = [pltpu.VMEM((d, d), BF16)] + ffn_scratch
    return pl.pallas_call(
        body,
        grid=(N_FFN_CHUNKS + t // FFN_TILE,),
        in_specs=specs + ffn_specs,
        out_specs=tile(d),
        out_shape=jax.ShapeDtypeStruct((t, d), F32),
        scratch_shapes=scratch,
        compiler_params=_params("arbitrary"),
        name="ffn" if mix is None else "mix_out_ffn",
    )(*args, *ffn_args)


def _causal_conv_tile(zpad_ref, w_ref, width, t0):
    zh = zpad_ref[t0:t0 + CONV_TILE + CONV_PAD, :]
    acc = None
    for r in range(min(SUBLANES, width)):
        zr = zh if r == 0 else pltpu.roll(zh, r, 0)
        for q in range((width - 1 - r) // SUBLANES + 1):
            j = width - 1 - (SUBLANES * q + r)
            start = CONV_PAD - SUBLANES * q
            term = zr[start:start + CONV_TILE] * w_ref[j:j + 1, :]
            acc = term if acc is None else acc + term
    return acc


_OFF_B = A_COLS
_OFF_C = A_COLS + B_COLS
_OFF_D = A_COLS + B_COLS + C_COLS
IN_CHUNK = 384
N_IN_CHUNKS = IN_COLS // IN_CHUNK


def _mix_in_body(x_ref, g_ref, w32_ref, *refs, tiles_per_seq):
    w_ref = refs[-3]
    step = pl.program_id(0)
    for c in range(N_IN_CHUNKS):
        @pl.when(step == c)
        def _(c=c):
            w_ref[:, c * IN_CHUNK:(c + 1) * IN_CHUNK] = w32_ref[0].astype(BF16)

    @pl.when(step >= N_IN_CHUNKS)
    def _():
        first = (step - N_IN_CHUNKS) % tiles_per_seq == 0
        _mix_in_tile(first, x_ref, g_ref, w_ref, *refs[:-3], *refs[-2:])


def _mix_in_tile(first, x_ref, g_ref, w_ref, scw_ref, sg_lnw_ref, sg_lnb_ref, sgw_ref, sgb_ref,
                 cmw_ref, cmb_ref, cm_lnw_ref, cm_lnb_ref,
                 ya_ref, yb_ref, pc_ref, yd_ref, za_ref, zd_ref):
    g = GROUP_W
    ts = x_ref.shape[0]

    @pl.when(first)
    def _():
        za_ref[0:CONV_PAD, :] = jnp.zeros((CONV_PAD, g), F32)
        zd_ref[0:CONV_PAD, :] = jnp.zeros((CONV_PAD, g), F32)

    @pl.when(jnp.logical_not(first))
    def _():
        za_ref[0:CONV_PAD, :] = za_ref[ts:ts + CONV_PAD, :]
        zd_ref[0:CONV_PAD, :] = zd_ref[ts:ts + CONV_PAD, :]

    h = _rms(x_ref[...], g_ref[...]).astype(BF16)
    z1 = jnp.dot(h, w_ref[:, _OFF_D:_OFF_D + g], preferred_element_type=F32)
    z2 = jnp.dot(h, w_ref[:, _OFF_D + g:_OFF_D + 2 * g], preferred_element_type=F32)
    zd_ref[CONV_PAD:, :] = z1 * _sigmoid(z2)
    pa = jnp.dot(h, w_ref[:, 0:A_COLS], preferred_element_type=F32)
    za_ref[CONV_PAD:, :] = pa[:, g:2 * g] * pa[:, 2 * g:3 * g]
    pb = jnp.dot(h, w_ref[:, _OFF_B:_OFF_B + B_COLS], preferred_element_type=F32)
    pc_ref[...] = jnp.dot(h, w_ref[:, _OFF_C:_OFF_C + C_COLS], preferred_element_type=F32)

    for t0 in range(0, ts, CONV_TILE):
        rows = slice(t0, t0 + CONV_TILE)
        y = _causal_conv_tile(zd_ref, cmw_ref, CM_WIDTH, t0) + cmb_ref[...]
        y = _layer_norm(y, cm_lnw_ref[...], cm_lnb_ref[...], LN_EPS)
        yd_ref[rows, :] = (y * _sigmoid(y)).astype(BF16)
        ya_ref[rows, :] = (pa[rows, 0:g] * _causal_conv_tile(za_ref, scw_ref, SC_WIDTH, t0)).astype(BF16)

    row = lax.broadcasted_iota(jnp.int32, (CHUNK, CHUNK), 0)
    col = lax.broadcasted_iota(jnp.int32, (CHUNK, CHUNK), 1)
    w_tril = jnp.concatenate([jnp.where(row >= col, sgw_ref[hd], 0.0).astype(BF16) for hd in range(N_HEADS)],
                             axis=1)
    lane_head = lax.broadcasted_iota(jnp.int32, (CHUNK, g), 1) // HEAD_DIM
    for t0 in range(0, ts, CHUNK):
        rows = slice(t0, t0 + CHUNK)
        v = _layer_norm(pb[rows, g:2 * g], sg_lnw_ref[...], sg_lnb_ref[...], LN_EPS).astype(BF16)
        v_heads = jnp.concatenate([jnp.where(lane_head == hd, v, jnp.zeros_like(v)) for hd in range(N_HEADS)],
                                  axis=0)
        s = sgb_ref[...] + jnp.dot(w_tril, v_heads, preferred_element_type=F32)
        yb_ref[rows, :] = (pb[rows, 0:g] * s).astype(BF16)


def _mix_in(x, seq, layer, g, w_in, sc_conv_w, sg_ln_w, sg_ln_b, sg_w, sg_b, cm_conv_w, cm_conv_b, cm_ln_w,
            cm_ln_b):
    t, d = x.shape
    gw = GROUP_W
    vec = lambda v: v.reshape(1, gw)
    sg_bias = jnp.repeat(sg_b.T, HEAD_DIM, axis=1)
    tile = lambda width: pl.BlockSpec((SEQ_TILE, width), lambda i: (jnp.maximum(i - N_IN_CHUNKS, 0), 0))
    w_chunk = pl.BlockSpec((1, d, IN_CHUNK), lambda i: (layer, 0, jnp.minimum(i, N_IN_CHUNKS - 1)))
    return pl.pallas_call(
        functools.partial(_mix_in_body, tiles_per_seq=seq // SEQ_TILE),
        grid=(N_IN_CHUNKS + t // SEQ_TILE,),
        in_specs=[tile(d), _resident((1, d)), w_chunk, _resident((SC_WIDTH, gw)),
                  _resident((1, gw)), _resident((1, gw)), _resident((N_HEADS, CHUNK, CHUNK)),
                  _resident((CHUNK, gw)), _resident((CM_WIDTH, gw)), _resident((1, gw)),
                  _resident((1, gw)), _resident((1, gw))],
        out_specs=[tile(gw), tile(gw), tile(C_COLS), tile(gw)],
        out_shape=[jax.ShapeDtypeStruct((t, gw), BF16), jax.ShapeDtypeStruct((t, gw), BF16),
                   jax.ShapeDtypeStruct((t, C_COLS), F32), jax.ShapeDtypeStruct((t, gw), BF16)],
        scratch_shapes=[pltpu.VMEM((d, IN_COLS), BF16), pltpu.VMEM((SEQ_TILE + CONV_PAD, gw), F32),
                        pltpu.VMEM((SEQ_TILE + CONV_PAD, gw), F32)],
        compiler_params=_params("arbitrary"),
        name="mix_in",
    )(x, g.reshape(1, d), w_in, sc_conv_w, vec(sg_ln_w), vec(sg_ln_b), sg_w, sg_bias,
      cm_conv_w, vec(cm_conv_b), vec(cm_ln_w), vec(cm_ln_b))


def _split2(x):
    hi = x.astype(BF16)
    lo = (x - hi.astype(F32)).astype(BF16)
    return hi, lo


def _head_sum(x, ones_bd):
    hi, lo = _split2(x)
    return (jnp.dot(hi, ones_bd, preferred_element_type=F32)
            + jnp.dot(lo, ones_bd, preferred_element_type=F32))


def _rwkv_body(pc_ref, mu_ref, w0_ref, lora_ref, a0_ref, kk_ref, ka_ref, rk_ref, lnw_ref, lnb_ref,
               o_ref, gm_ref, yc_ref, rec_ref, bonus_ref, gate_ref, state_ref, prev_ref):
    g = GROUP_W
    lc = RWKV_CHUNK
    tb = RWKV_BLOCK * lc
    seq = pc_ref.shape[1]

    row = lax.broadcasted_iota(jnp.int32, (lc, 2 * lc), 0)
    col = lax.broadcasted_iota(jnp.int32, (lc, 2 * lc), 1) % lc
    strict2 = row > col
    incl2 = row >= col
    brow = lax.broadcasted_iota(jnp.int32, (tb, tb), 0)
    bcol = lax.broadcasted_iota(jnp.int32, (tb, tb), 1)
    tri_ones = jnp.where((brow // lc == bcol // lc) & (brow >= bcol), 1.0, 0.0).astype(BF16)
    bd_r = lax.broadcasted_iota(jnp.int32, (g, g), 0) // HEAD_DIM
    bd_c = lax.broadcasted_iota(jnp.int32, (g, g), 1) // HEAD_DIM
    ones_bd = jnp.where(bd_r == bd_c, 1.0, 0.0).astype(BF16)
    pr = lax.broadcasted_iota(jnp.int32, (PAIR_W, PAIR_W), 0)
    pcol = lax.broadcasted_iota(jnp.int32, (PAIR_W, PAIR_W), 1)
    pair_bd = (pr // HEAD_DIM) == (pcol // HEAD_DIM)
    pair_eye = pr == pcol
    first_head = lax.broadcasted_iota(jnp.int32, (lc, PAIR_W), 1) < HEAD_DIM
    first_head2 = jnp.concatenate([first_head, first_head], axis=1)
    row0 = lax.broadcasted_iota(jnp.int32, (tb, C_COLS), 0) == 0

    def split(y):
        yb = y.astype(BF16)
        m = first_head if y.shape[1] == PAIR_W else first_head2
        zero = jnp.zeros_like(yb)
        return jnp.concatenate([jnp.where(m, yb, zero), jnp.where(m, zero, yb)], axis=0)

    state_ref[...] = jnp.zeros(state_ref.shape, F32)
    prev_ref[...] = jnp.zeros(prev_ref.shape, F32)

    def build(i):
        t0 = pl.multiple_of(i * tb, tb)
        x = pc_ref[0, pl.ds(t0, tb), :]
        xs = jnp.where(row0, prev_ref[...], pltpu.roll(x, 1, 0))
        prev_ref[...] = x[tb - 1:tb, :]
        xm = x + (xs - x) * mu_ref[...]
        r = xm[:, 0:g]
        k = xm[:, g:2 * g]
        v = xm[:, 2 * g:3 * g]
        lo_in = xm[:, 3 * g:3 * g + LORA_W]
        lane = lax.broadcasted_iota(jnp.int32, (tb, LORA_W), 1)
        lo_act = jnp.where(lane < RANK_W, jnp.tanh(lo_in),
                           jnp.where(lane < RANK_W + RANK_A, lo_in, _sigmoid(lo_in)))
        lora = _mm(lo_act, lora_ref[...])
        e = _sigmoid(w0_ref[...] + lora[:, 0:g]) * math.exp(-0.5)
        a_sig = _sigmoid(a0_ref[...] + lora[:, g:2 * g])
        gate_ref[pl.ds(t0, tb), :] = lora[:, 2 * g:3 * g]
        kk = k * kk_ref[...]
        kk = kk * jnp.minimum(lax.rsqrt(_head_sum(kk * kk, ones_bd)), 1e12)
        k = k * (1.0 + (a_sig - 1.0) * ka_ref[...])
        a_ = -kk
        b_ = kk * a_sig
        bonus_ref[pl.ds(t0, tb), :] = _head_sum(r * k * rk_ref[...], ones_bd) * v

        e_hi = e.astype(BF16)
        e_r = e - e_hi.astype(F32)
        e_mid = e_r.astype(BF16)
        e_lo = (e_r - e_mid.astype(F32)).astype(BF16)
        cs = (jnp.dot(tri_ones, e_hi, preferred_element_type=F32)
              + jnp.dot(tri_ones, e_mid, preferred_element_type=F32)
              + jnp.dot(tri_ones, e_lo, preferred_element_type=F32))
        cs_last = jnp.concatenate(
            [jnp.broadcast_to(cs[(ci + 1) * lc - 1:(ci + 1) * lc, :], (lc, g)) for ci in range(RWKV_BLOCK)],
            axis=0)
        w_inc = jnp.exp(-cs)
        w_exc = jnp.exp(e - cs)
        w_inv = jnp.exp(cs)
        w_fin = jnp.exp(cs - cs_last)
        w_chunk_all = jnp.exp(-cs_last)
        at_all = a_ * w_exc
        rt_all = r * w_inc
        bt_all = b_ * w_inv
        kt_all = k * w_inv
        bh_all = b_ * w_fin
        kh_all = k * w_fin
        yield

        probs = [(ci, p) for ci in range(RWKV_BLOCK) for p in range(N_HEADS // 2)]
        ops = []
        for ci, p in probs:
            rs = slice(ci * lc, (ci + 1) * lc)
            ps = slice(p * PAIR_W, (p + 1) * PAIR_W)
            ops.append(tuple(t[rs, ps] for t in (at_all, rt_all, bt_all, kt_all, bh_all, kh_all, v)))
        a_ab, a_ak, a_rb, a_rk = [], [], [], []
        for at, rt, bt, kt, bh, kh, vp in ops:
            lhs = jnp.concatenate([at, rt], axis=0)
            ab = _mm_nt(lhs, split(bt))
            ak = _mm_nt(lhs, split(kt))
            a_ab.append(jnp.where(strict2, ab[0:lc], 0.0))
            a_ak.append(jnp.where(strict2, ak[0:lc], 0.0))
            a_rb.append(jnp.where(incl2, ab[lc:], 0.0))
            a_rk.append(jnp.where(incl2, ak[lc:], 0.0))
        yield
        zs = [jnp.concatenate([op[0], _mm(a, split(op[6]))], axis=1) for a, op in zip(a_ak, ops)]
        yield
        aps = list(a_ab)
        n_fac = int(math.log2(lc))
        for f in range(n_fac):
            zs = [z + _mm(ap, split(z)) for ap, z in zip(aps, zs)]
            if f + 1 < n_fac:
                aps = [_mm(ap, split(ap)) for ap in aps]
            yield
        for n, ((ci, p), (at, rt, bt, kt, bh, kh, vp)) in enumerate(zip(probs, ops)):
            ps = slice(p * PAIR_W, (p + 1) * PAIR_W)
            idx = i * RWKV_BLOCK + ci
            z = zs[n]
            gy = _mm(a_rb[n], split(z))
            g_mat = rt + gy[:, 0:PAIR_W]
            y0 = gy[:, PAIR_W:] + _mm(a_rk[n], split(vp))
            mc = _mm_tn(bh, z)
            kv = _mm_tn(kh, vp)
            w_chunk = w_chunk_all[ci * lc:ci * lc + 1, ps]
            m_mat = jnp.where(pair_bd, mc[:, 0:PAIR_W], 0.0) + jnp.where(pair_eye, w_chunk, 0.0)
            c_mat = jnp.where(pair_bd, mc[:, PAIR_W:] + kv, 0.0)
            gm_ref[idx, p, 0:lc, :] = g_mat.astype(BF16)
            gm_ref[idx, p, lc:, :] = m_mat.astype(BF16)
            yc_ref[idx, p, 0:lc, :] = y0
            yc_ref[idx, p, lc:, :] = c_mat

    def advance(i):
        for ci in range(RWKV_BLOCK):
            c = i * RWKV_BLOCK + ci
            t0 = pl.multiple_of(c * lc, lc)
            for p in range(N_HEADS // 2):
                step = jnp.dot(gm_ref[c, p], state_ref[p].astype(BF16), preferred_element_type=F32)
                step = step + yc_ref[c, p]
                rec_ref[pl.ds(t0, lc), p * PAIR_W:(p + 1) * PAIR_W] = step[0:lc]
                state_ref[p] = step[lc:]
            yield
        t0 = pl.multiple_of(i * tb, tb)
        o = rec_ref[pl.ds(t0, tb), :]
        mean = _head_sum(o, ones_bd) * (1.0 / HEAD_DIM)
        yield
        oc = o - mean
        var = _head_sum(oc * oc, ones_bd) * (1.0 / HEAD_DIM)
        yield
        o = oc * lax.rsqrt(var + GN_EPS) * lnw_ref[...] + lnb_ref[...]
        o = (o + bonus_ref[pl.ds(t0, tb), :]) * gate_ref[pl.ds(t0, tb), :]
        o_ref[0, pl.ds(t0, tb), :] = o.astype(BF16)

    n_blocks = seq // tb
    _interleave(build(0))

    def body(i, carry):
        _interleave(build(i), advance(i - 1))
        return carry

    lax.fori_loop(1, n_blocks, body, 0)
    _interleave(advance(n_blocks - 1))


def _mix_c(pc, mu, w0, w_up, a0, a_up, g_up, k_k, k_a, r_k, ln_w, ln_b):
    b, s, _ = pc.shape
    g = GROUP_W
    n_pairs = N_HEADS // 2
    lora = jnp.zeros((LORA_W, 3 * g), F32)
    lora = lora.at[0:RANK_W, 0:g].set(w_up)
    lora = lora.at[RANK_W:RANK_W + RANK_A, g:2 * g].set(a_up)
    lora = lora.at[RANK_W + RANK_A:, 2 * g:].set(g_up)
    vec = lambda t: t.reshape(1, g)
    return pl.pallas_call(
        _rwkv_body,
        grid=(b,),
        in_specs=[pl.BlockSpec((1, s, C_COLS), lambda i: (i, 0, 0)), _resident((1, C_COLS)),
                  _resident((1, g)), _resident((LORA_W, 3 * g)), _resident((1, g)), _resident((1, g)),
                  _resident((1, g)), _resident((1, g)), _resident((1, g)), _resident((1, g))],
        out_specs=pl.BlockSpec((1, s, g), lambda i: (i, 0, 0)),
        out_shape=jax.ShapeDtypeStruct((b, s, g), BF16),
        scratch_shapes=[
            pltpu.VMEM((s // RWKV_CHUNK, n_pairs, RWKV_CHUNK + PAIR_W, PAIR_W), BF16),
            pltpu.VMEM((s // RWKV_CHUNK, n_pairs, RWKV_CHUNK + PAIR_W, PAIR_W), F32),
            pltpu.VMEM((s, g), F32),
            pltpu.VMEM((s, g), F32),
            pltpu.VMEM((s, g), F32),
            pltpu.VMEM((n_pairs, PAIR_W, PAIR_W), F32),
            pltpu.VMEM((1, C_COLS), F32),
        ],
        compiler_params=_params("parallel"),
        name="mix_c",
    )(pc, mu.reshape(1, C_COLS), vec(w0), lora.astype(BF16), vec(a0), vec(k_k), vec(k_a), vec(r_k),
      vec(ln_w), vec(ln_b))


def kernel(x, ffn1_pre_g, ffn1_w_gate, ffn1_w_up, ffn1_w_down, ffn1_post_g, mix_pre_g, w_in, sc_conv_w, sg_ln_w, sg_ln_b, sg_w, sg_b, rk_mu, rk_w0, rk_w_up, rk_a0, rk_a_up, rk_g_up, rk_k_k, rk_k_a, rk_r_k, rk_ln_w, rk_ln_b, cm_conv_w, cm_conv_b, cm_ln_w, cm_ln_b, w_out, mix_post_g, ffn2_pre_g, ffn2_w_gate, ffn2_w_up, ffn2_w_down, ffn2_post_g):
    b, s, d = x.shape
    t = b * s
    xf = x.reshape(t, d)
    for l in range(ffn1_pre_g.shape[0]):
        xf = _ffn(xf, l, ffn1_pre_g, ffn1_w_gate, ffn1_w_up, ffn1_w_down, ffn1_post_g)
        ya, yb, pc, yd = _mix_in(xf, s, l, mix_pre_g[l], w_in, sc_conv_w[l], sg_ln_w[l], sg_ln_b[l],
                                 sg_w[l], sg_b[l], cm_conv_w[l], cm_conv_b[l], cm_ln_w[l], cm_ln_b[l])
        yc = _mix_c(pc.reshape(b, s, C_COLS), rk_mu[l], rk_w0[l], rk_w_up[l], rk_a0[l], rk_a_up[l],
                    rk_g_up[l], rk_k_k[l], rk_k_a[l], rk_r_k[l].reshape(-1), rk_ln_w[l], rk_ln_b[l])
        xf = _ffn(xf, l, ffn2_pre_g, ffn2_w_gate, ffn2_w_up, ffn2_w_down, ffn2_post_g,
                  mix=((ya, yb, yc.reshape(t, GROUP_W), yd), w_out, mix_post_g))
    return xf.reshape(b, s, d)
```

```python
import functools
import math

import jax
import jax.numpy as jnp
from jax import lax
from jax.experimental import pallas as pl
from jax.experimental.pallas import tpu as pltpu

F32 = jnp.float32
BF16 = jnp.bfloat16

D_MODEL = 1024
D_FF = 2816
GROUP_W = 256
HEAD_DIM = 64
N_HEADS = 4
SC_WIDTH = 3
CHUNK = 128
CM_WIDTH = 31
RANK_W = 32
RANK_A = 32
RANK_G = 64
A_COLS = 3 * GROUP_W
B_COLS = 2 * GROUP_W
C_COLS = 3 * GROUP_W + RANK_W + RANK_A + RANK_G
D_COLS = 2 * GROUP_W
IN_COLS = A_COLS + B_COLS + C_COLS + D_COLS
LORA_W = RANK_W + RANK_A + RANK_G
RMS_EPS = 1e-6
LN_EPS = 1e-5
GN_EPS = 1e-5 * HEAD_DIM
FFN_RESID = 0.5

V7X_VMEM_BYTES = 64 * 1024 * 1024
VMEM_LIMIT_BYTES = V7X_VMEM_BYTES - 8 * 1024 * 1024
SUBLANES = 8

FFN_TILE = 512
SEQ_TILE = 1024
FFN_CHUNK = 256
N_FFN_CHUNKS = D_FF // FFN_CHUNK
RWKV_CHUNK = 64
RWKV_BLOCK = 4
CONV_PAD = 32
CONV_TILE = 128
PAIR_W = 2 * HEAD_DIM
assert CONV_PAD >= CM_WIDTH - 1 and CONV_PAD % SUBLANES == 0


def _params(*sem):
    return pltpu.CompilerParams(dimension_semantics=sem, vmem_limit_bytes=VMEM_LIMIT_BYTES)


def _resident(shape):
    nd = len(shape)
    return pl.BlockSpec(shape, lambda *_: (0,) * nd, pipeline_mode=pl.Buffered(1))


def _mm(a, b):
    return jnp.dot(a.astype(BF16), b.astype(BF16), preferred_element_type=F32)


def _mm_nt(a, b):
    return lax.dot_general(a.astype(BF16), b.astype(BF16), (((1,), (1,)), ((), ())),
                           preferred_element_type=F32)


def _mm_tn(a, b):
    return lax.dot_general(a.astype(BF16), b.astype(BF16), (((0,), (0,)), ((), ())),
                           preferred_element_type=F32)


def _interleave(*streams):
    live = list(streams)
    while live:
        live = [s for s in live if next(s, live) is not live]


def _sigmoid(x):
    return 0.5 * jnp.tanh(0.5 * x) + 0.5


def _rms(x, g):
    return x * lax.rsqrt(jnp.mean(x * x, axis=-1, keepdims=True) + RMS_EPS) * g


def _layer_norm(x, g, b, eps):
    mu = jnp.mean(x, axis=-1, keepdims=True)
    xc = x - mu
    var = jnp.mean(xc * xc, axis=-1, keepdims=True)
    return xc * lax.rsqrt(var + eps) * g + b


def _ffn_tail(x, pre_g_ref, wg_ref, wu_ref, wd_ref, post_g_ref, o_ref, acc_ref):
    h = _rms(x, pre_g_ref[...]).astype(BF16)
    for c in range(D_FF // FFN_CHUNK):
        sl = slice(c * FFN_CHUNK, (c + 1) * FFN_CHUNK)
        g = jnp.dot(h, wg_ref[:, sl], preferred_element_type=F32)
        u = jnp.dot(h, wu_ref[:, sl], preferred_element_type=F32)
        act = (g * _sigmoid(g) * u).astype(BF16)
        part = jnp.dot(act, wd_ref[sl, :], preferred_element_type=F32)
        if c == 0:
            acc_ref[...] = part
        else:
            acc_ref[...] += part
    o_ref[...] = x + FFN_RESID * _rms(acc_ref[...], post_g_ref[...])


def _ffn_phases(step, x_fn, pre_g_ref, wg32_ref, wu32_ref, wd32_ref, post_g_ref, o_ref,
                wg_ref, wu_ref, wd_ref, acc_ref):
    for c in range(N_FFN_CHUNKS):
        @pl.when(step == c)
        def _(c=c):
            sl = slice(c * FFN_CHUNK, (c + 1) * FFN_CHUNK)
            wg_ref[:, sl] = wg32_ref[0].astype(BF16)
            wu_ref[:, sl] = wu32_ref[0].astype(BF16)
            wd_ref[sl, :] = wd32_ref[0].astype(BF16)

    @pl.when(step >= N_FFN_CHUNKS)
    def _():
        _ffn_tail(x_fn(), pre_g_ref.at[0], wg_ref, wu_ref, wd_ref, post_g_ref.at[0], o_ref, acc_ref)


def _ffn_body(x_ref, *refs):
    _ffn_phases(pl.program_id(0), lambda: x_ref[...], *refs)


def _mix_out_ffn_body(x_ref, ya_ref, yb_ref, yc_ref, yd_ref, wo32_ref, mix_g_ref, *refs):
    wo_ref, ffn_refs = refs[-5], refs[:-5] + refs[-4:]
    step = pl.program_id(0)

    @pl.when(step == 0)
    def _():
        wo_ref[...] = wo32_ref[0].astype(BF16)

    def mixed():
        m = None
        for i, y_ref in enumerate((ya_ref, yb_ref, yc_ref, yd_ref)):
            part = jnp.dot(y_ref[...], wo_ref[i * GROUP_W:(i + 1) * GROUP_W, :], preferred_element_type=F32)
            m = part if m is None else m + part
        return x_ref[...] + _rms(m, mix_g_ref[0])

    _ffn_phases(step, mixed, *ffn_refs)


def _ffn(x, layer, pre_g, wg, wu, wd, post_g, mix=None):
    t, d = x.shape
    n_layers, _, f = wg.shape
    last = N_FFN_CHUNKS - 1
    tile = lambda width: pl.BlockSpec((FFN_TILE, width), lambda i: (jnp.maximum(i - N_FFN_CHUNKS, 0), 0))
    gain = pl.BlockSpec((1, 1, d), lambda i: (layer, 0, 0))
    gains = lambda g: g.reshape(n_layers, 1, d)
    ffn_specs = [gain,
                 pl.BlockSpec((1, d, FFN_CHUNK), lambda i: (layer, 0, jnp.minimum(i, last))),
                 pl.BlockSpec((1, d, FFN_CHUNK), lambda i: (layer, 0, jnp.minimum(i, last))),
                 pl.BlockSpec((1, FFN_CHUNK, d), lambda i: (layer, jnp.minimum(i, last), 0)),
                 gain]
    ffn_args = (gains(pre_g), wg, wu, wd, gains(post_g))
    ffn_scratch = [pltpu.VMEM((d, f), BF16), pltpu.VMEM((d, f), BF16), pltpu.VMEM((f, d), BF16),
                   pltpu.VMEM((FFN_TILE, d), F32)]
    if mix is None:
        body, specs, args, scratch = _ffn_body, [tile(d)], (x,), ffn_scratch
    else:
        ys, w_out, mix_g = mix
        body = _mix_out_ffn_body
        specs = ([tile(d)] + [tile(GROUP_W)] * len(ys)
                 + [pl.BlockSpec((1, d, d), lambda i: (layer, 0, 0), pipeline_mode=pl.Buffered(1)), gain])
        args = (x, *ys, w_out, gains(mix_g))
        scratch = [pltpu.VMEM((d, d), BF16)] + ffn_scratch
    return pl.pallas_call(
        body,
        grid=(N_FFN_CHUNKS + t // FFN_TILE,),
        in_specs=specs + ffn_specs,
        out_specs=tile(d),
        out_shape=jax.ShapeDtypeStruct((t, d), F32),
        scratch_shapes=scratch,
        compiler_params=_params("arbitrary"),
        name="ffn" if mix is None else "mix_out_ffn",
    )(*args, *ffn_args)


def _causal_conv_tile(zpad_ref, w_ref, width, t0):
    zh = zpad_ref[t0:t0 + CONV_TILE + CONV_PAD, :]
    acc = None
    for r in range(min(SUBLANES, width)):
        zr = zh if r == 0 else pltpu.roll(zh, r, 0)
        for q in range((width - 1 - r) // SUBLANES + 1):
            j = width - 1 - (SUBLANES * q + r)
            start = CONV_PAD - SUBLANES * q
            term = zr[start:start + CONV_TILE] * w_ref[j:j + 1, :]
            acc = term if acc is None else acc + term
    return acc


_OFF_B = A_COLS
_OFF_C = A_COLS + B_COLS
_OFF_D = A_COLS + B_COLS + C_COLS
IN_CHUNK = 384
N_IN_CHUNKS = IN_COLS // IN_CHUNK


def _mix_in_body(x_ref, g_ref, w32_ref, *refs, tiles_per_seq):
    w_ref = refs[-3]
    step = pl.program_id(0)
    for c in range(N_IN_CHUNKS):
        @pl.when(step == c)
        def _(c=c):
            w_ref[:, c * IN_CHUNK:(c + 1) * IN_CHUNK] = w32_ref[0].astype(BF16)

    @pl.when(step >= N_IN_CHUNKS)
    def _():
        first = (step - N_IN_CHUNKS) % tiles_per_seq == 0
        _mix_in_tile(first, x_ref, g_ref, w_ref, *refs[:-3], *refs[-2:])


def _mix_in_tile(first, x_ref, g_ref, w_ref, scw_ref, sg_lnw_ref, sg_lnb_ref, sgw_ref, sgb_ref,
                 cmw_ref, cmb_ref, cm_lnw_ref, cm_lnb_ref,
                 ya_ref, yb_ref, pc_ref, yd_ref, za_ref, zd_ref):
    g = GROUP_W
    ts = x_ref.shape[0]

    @pl.when(first)
    def _():
        za_ref[0:CONV_PAD, :] = jnp.zeros((CONV_PAD, g), F32)
        zd_ref[0:CONV_PAD, :] = jnp.zeros((CONV_PAD, g), F32)

    @pl.when(jnp.logical_not(first))
    def _():
        za_ref[0:CONV_PAD, :] = za_ref[ts:ts + CONV_PAD, :]
        zd_ref[0:CONV_PAD, :] = zd_ref[ts:ts + CONV_PAD, :]

    h = _rms(x_ref[...], g_ref[...]).astype(BF16)
    z1 = jnp.dot(h, w_ref[:, _OFF_D:_OFF_D + g], preferred_element_type=F32)
    z2 = jnp.dot(h, w_ref[:, _OFF_D + g:_OFF_D + 2 * g], preferred_element_type=F32)
    zd_ref[CONV_PAD:, :] = z1 * _sigmoid(z2)
    pa = jnp.dot(h, w_ref[:, 0:A_COLS], preferred_element_type=F32)
    za_ref[CONV_PAD:, :] = pa[:, g:2 * g] * pa[:, 2 * g:3 * g]
    pb = jnp.dot(h, w_ref[:, _OFF_B:_OFF_B + B_COLS], preferred_element_type=F32)
    pc_ref[...] = jnp.dot(h, w_ref[:, _OFF_C:_OFF_C + C_COLS], preferred_element_type=F32)

    for t0 in range(0, ts, CONV_TILE):
        rows = slice(t0, t0 + CONV_TILE)
        y = _causal_conv_tile(zd_ref, cmw_ref, CM_WIDTH, t0) + cmb_ref[...]
        y = _layer_norm(y, cm_lnw_ref[...], cm_lnb_ref[...], LN_EPS)
        yd_ref[rows, :] = (y * _sigmoid(y)).astype(BF16)
        ya_ref[rows, :] = (pa[rows, 0:g] * _causal_conv_tile(za_ref, scw_ref, SC_WIDTH, t0)).astype(BF16)

    row = lax.broadcasted_iota(jnp.int32, (CHUNK, CHUNK), 0)
    col = lax.broadcasted_iota(jnp.int32, (CHUNK, CHUNK), 1)
    w_tril = jnp.concatenate([jnp.where(row >= col, sgw_ref[hd], 0.0).astype(BF16) for hd in range(N_HEADS)],
                             axis=1)
    lane_head = lax.broadcasted_iota(jnp.int32, (CHUNK, g), 1) // HEAD_DIM
    for t0 in range(0, ts, CHUNK):
        rows = slice(t0, t0 + CHUNK)
        v = _layer_norm(pb[rows, g:2 * g], sg_lnw_ref[...], sg_lnb_ref[...], LN_EPS).astype(BF16)
        v_heads = jnp.concatenate([jnp.where(lane_head == hd, v, jnp.zeros_like(v)) for hd in range(N_HEADS)],
                                  axis=0)
        s = sgb_ref[...] + jnp.dot(w_tril, v_heads, preferred_element_type=F32)
        yb_ref[rows, :] = (pb[rows, 0:g] * s).astype(BF16)


def _mix_in(x, seq, layer, g, w_in, sc_conv_w, sg_ln_w, sg_ln_b, sg_w, sg_b, cm_conv_w, cm_conv_b, cm_ln_w,
            cm_ln_b):
    t, d = x.shape
    gw = GROUP_W
    vec = lambda v: v.reshape(1, gw)
    sg_bias = jnp.repeat(sg_b.T, HEAD_DIM, axis=1)
    tile = lambda width: pl.BlockSpec((SEQ_TILE, width), lambda i: (jnp.maximum(i - N_IN_CHUNKS, 0), 0))
    w_chunk = pl.BlockSpec((1, d, IN_CHUNK), lambda i: (layer, 0, jnp.minimum(i, N_IN_CHUNKS - 1)))
    return pl.pallas_call(
        functools.partial(_mix_in_body, tiles_per_seq=seq // SEQ_TILE),
        grid=(N_IN_CHUNKS + t // SEQ_TILE,),
        in_specs=[tile(d), _resident((1, d)), w_chunk, _resident((SC_WIDTH, gw)),
                  _resident((1, gw)), _resident((1, gw)), _resident((N_HEADS, CHUNK, CHUNK)),
                  _resident((CHUNK, gw)), _resident((CM_WIDTH, gw)), _resident((1, gw)),
                  _resident((1, gw)), _resident((1, gw))],
        out_specs=[tile(gw), tile(gw), tile(C_COLS), tile(gw)],
        out_shape=[jax.ShapeDtypeStruct((t, gw), BF16), jax.ShapeDtypeStruct((t, gw), BF16),
                   jax.ShapeDtypeStruct((t, C_COLS), F32), jax.ShapeDtypeStruct((t, gw), BF16)],
        scratch_shapes=[pltpu.VMEM((d, IN_COLS), BF16), pltpu.VMEM((SEQ_TILE + CONV_PAD, gw), F32),
                        pltpu.VMEM((SEQ_TILE + CONV_PAD, gw), F32)],
        compiler_params=_params("arbitrary"),
        name="mix_in",
    )(x, g.reshape(1, d), w_in, sc_conv_w, vec(sg_ln_w), vec(sg_ln_b), sg_w, sg_bias,
      cm_conv_w, vec(cm_conv_b), vec(cm_ln_w), vec(cm_ln_b))


def _split2(x):
    hi = x.astype(BF16)
    lo = (x - hi.astype(F32)).astype(BF16)
    return hi, lo


def _head_sum(x, ones_bd):
    hi, lo = _split2(x)
    return (jnp.dot(hi, ones_bd, preferred_element_type=F32)
            + jnp.dot(lo, ones_bd, preferred_element_type=F32))


def _rwkv_body(pc_ref, mu_ref, w0_ref, lora_ref, a0_ref, kk_ref, ka_ref, rk_ref, lnw_ref, lnb_ref,
               o_ref, gm_ref, yc_ref, rec_ref, bonus_ref, gate_ref, state_ref, prev_ref):
    g = GROUP_W
    lc = RWKV_CHUNK
    tb = RWKV_BLOCK * lc
    seq = pc_ref.shape[1]

    row = lax.broadcasted_iota(jnp.int32, (lc, 2 * lc), 0)
    col = lax.broadcasted_iota(jnp.int32, (lc, 2 * lc), 1) % lc
    strict2 = row > col
    incl2 = row >= col
    brow = lax.broadcasted_iota(jnp.int32, (tb, tb), 0)
    bcol = lax.broadcasted_iota(jnp.int32, (tb, tb), 1)
    tri_ones = jnp.where((brow // lc == bcol // lc) & (brow >= bcol), 1.0, 0.0).astype(BF16)
    bd_r = lax.broadcasted_iota(jnp.int32, (g, g), 0) // HEAD_DIM
    bd_c = lax.broadcasted_iota(jnp.int32, (g, g), 1) // HEAD_DIM
    ones_bd = jnp.where(bd_r == bd_c, 1.0, 0.0).astype(BF16)
    pr = lax.broadcasted_iota(jnp.int32, (PAIR_W, PAIR_W), 0)
    pcol = lax.broadcasted_iota(jnp.int32, (PAIR_W, PAIR_W), 1)
    pair_bd = (pr // HEAD_DIM) == (pcol // HEAD_DIM)
    pair_eye = pr == pcol
    first_head = lax.broadcasted_iota(jnp.int32, (lc, PAIR_W), 1) < HEAD_DIM
    first_head2 = jnp.concatenate([first_head, first_head], axis=1)
    row0 = lax.broadcasted_iota(jnp.int32, (tb, C_COLS), 0) == 0
    pad_v = jnp.zeros((lc, PAIR_W), BF16)

    def split(y):
        yb = y.astype(BF16)
        m = first_head if y.shape[1] == PAIR_W else first_head2
        zero = jnp.zeros_like(yb)
        return jnp.concatenate([jnp.where(m, yb, zero), jnp.where(m, zero, yb)], axis=0)

    state_ref[...] = jnp.zeros(state_ref.shape, F32)
    prev_ref[...] = jnp.zeros(prev_ref.shape, F32)

    def build(i):
        t0 = pl.multiple_of(i * tb, tb)
        x = pc_ref[0, pl.ds(t0, tb), :]
        xs = jnp.where(row0, prev_ref[...], pltpu.roll(x, 1, 0))
        prev_ref[...] = x[tb - 1:tb, :]
        xm = x + (xs - x) * mu_ref[...]
        r = xm[:, 0:g]
        k = xm[:, g:2 * g]
        v = xm[:, 2 * g:3 * g]
        lo_in = xm[:, 3 * g:3 * g + LORA_W]
        lane = lax.broadcasted_iota(jnp.int32, (tb, LORA_W), 1)
        lo_act = jnp.where(lane < RANK_W, jnp.tanh(lo_in),
                           jnp.where(lane < RANK_W + RANK_A, lo_in, _sigmoid(lo_in)))
        lora = _mm(lo_act, lora_ref[...])
        e = _sigmoid(w0_ref[...] + lora[:, 0:g]) * math.exp(-0.5)
        a_sig = _sigmoid(a0_ref[...] + lora[:, g:2 * g])
        gate_ref[pl.ds(t0, tb), :] = lora[:, 2 * g:3 * g]
        kk = k * kk_ref[...]
        kk = kk * jnp.minimum(lax.rsqrt(_head_sum(kk * kk, ones_bd)), 1e12)
        k = k * (1.0 + (a_sig - 1.0) * ka_ref[...])
        a_ = -kk
        b_ = kk * a_sig
        bonus_ref[pl.ds(t0, tb), :] = _head_sum(r * k * rk_ref[...], ones_bd) * v

        e_hi = e.astype(BF16)
        e_r = e - e_hi.astype(F32)
        e_mid = e_r.astype(BF16)
        e_lo = (e_r - e_mid.astype(F32)).astype(BF16)
        cs = (jnp.dot(tri_ones, e_hi, preferred_element_type=F32)
              + jnp.dot(tri_ones, e_mid, preferred_element_type=F32)
              + jnp.dot(tri_ones, e_lo, preferred_element_type=F32))
        cs_last = jnp.concatenate(
            [jnp.broadcast_to(cs[(ci + 1) * lc - 1:(ci + 1) * lc, :], (lc, g)) for ci in range(RWKV_BLOCK)],
            axis=0)
        w_inc = jnp.exp(-cs)
        w_exc = jnp.exp(e - cs)
        w_inv = jnp.exp(cs)
        w_fin = jnp.exp(cs - cs_last)
        w_chunk_all = jnp.exp(-cs_last)
        at_all = a_ * w_exc
        rt_all = r * w_inc
        bt_all = b_ * w_inv
        kt_all = k * w_inv
        bh_all = b_ * w_fin
        kh_all = k * w_fin
        yield

        probs = [(ci, p) for ci in range(RWKV_BLOCK) for p in range(N_HEADS // 2)]
        ops = []
        for ci, p in probs:
            rs = slice(ci * lc, (ci + 1) * lc)
            ps = slice(p * PAIR_W, (p + 1) * PAIR_W)
            ops.append(tuple(t[rs, ps] for t in (at_all, rt_all, bt_all, kt_all, bh_all, kh_all, v)))
        a_ab, a_ak, a_rb, a_rk = [], [], [], []
        for at, rt, bt, kt, bh, kh, vp in ops:
            lhs = jnp.concatenate([at, rt], axis=0)
            abk = _mm_nt(lhs, jnp.concatenate([split(bt), split(kt)], axis=0))
            ab, ak = abk[:, 0:2 * lc], abk[:, 2 * lc:]
            a_ab.append(jnp.where(strict2, ab[0:lc], 0.0))
            a_ak.append(jnp.where(strict2, ak[0:lc], 0.0))
            a_rb.append(jnp.where(incl2, ab[lc:], 0.0))
            a_rk.append(jnp.where(incl2, ak[lc:], 0.0))
        yield
        zs = [jnp.concatenate([op[0], _mm(a, split(op[6]))], axis=1) for a, op in zip(a_ak, ops)]
        yield
        aps = list(a_ab)
        n_fac = int(math.log2(lc))
        for f in range(n_fac):
            if f + 1 < n_fac:
                both = [_mm(ap, jnp.concatenate([split(z), split(ap)], axis=1)) for ap, z in zip(aps, zs)]
                zs = [z + b[:, 0:2 * PAIR_W] for z, b in zip(zs, both)]
                aps = [b[:, 2 * PAIR_W:] for b in both]
            else:
                zs = [z + _mm(ap, split(z)) for ap, z in zip(aps, zs)]
            yield
        for n, ((ci, p), (at, rt, bt, kt, bh, kh, vp)) in enumerate(zip(probs, ops)):
            ps = slice(p * PAIR_W, (p + 1) * PAIR_W)
            idx = i * RWKV_BLOCK + ci
            z = zs[n]
            zv = jnp.concatenate([z.astype(BF16), jnp.concatenate([pad_v, vp.astype(BF16)], axis=1)],
                                 axis=0)
            gy = _mm(jnp.concatenate([a_rb[n], a_rk[n]], axis=1),
                     jnp.concatenate([split(zv[0:lc]), split(zv[lc:])], axis=0))
            g_mat = rt + gy[:, 0:PAIR_W]
            y0 = gy[:, PAIR_W:]
            mc = _mm_tn(jnp.concatenate([bh, kh], axis=0), zv)
            w_chunk = w_chunk_all[ci * lc:ci * lc + 1, ps]
            m_mat = jnp.where(pair_bd, mc[:, 0:PAIR_W], 0.0) + jnp.where(pair_eye, w_chunk, 0.0)
            c_mat = jnp.where(pair_bd, mc[:, PAIR_W:], 0.0)
            gm_ref[idx, p, 0:lc, :] = g_mat.astype(BF16)
            gm_ref[idx, p, lc:, :] = m_mat.astype(BF16)
            yc_ref[idx, p, 0:lc, :] = y0
            yc_ref[idx, p, lc:, :] = c_mat

    def advance(i):
        for ci in range(RWKV_BLOCK):
            c = i * RWKV_BLOCK + ci
            t0 = pl.multiple_of(c * lc, lc)
            for p in range(N_HEADS // 2):
                step = jnp.dot(gm_ref[c, p], state_ref[p].astype(BF16), preferred_element_type=F32)
                step = step + yc_ref[c, p]
                rec_ref[pl.ds(t0, lc), p * PAIR_W:(p + 1) * PAIR_W] = step[0:lc]
                state_ref[p] = step[lc:]
            yield
        t0 = pl.multiple_of(i * tb, tb)
        o = rec_ref[pl.ds(t0, tb), :]
        mean = _head_sum(o, ones_bd) * (1.0 / HEAD_DIM)
        yield
        oc = o - mean
        var = _head_sum(oc * oc, ones_bd) * (1.0 / HEAD_DIM)
        yield
        o = oc * lax.rsqrt(var + GN_EPS) * lnw_ref[...] + lnb_ref[...]
        o = (o + bonus_ref[pl.ds(t0, tb), :]) * gate_ref[pl.ds(t0, tb), :]
        o_ref[0, pl.ds(t0, tb), :] = o.astype(BF16)

    n_blocks = seq // tb
    _interleave(build(0))

    def body(i, carry):
        _interleave(build(i), advance(i - 1))
        return carry

    lax.fori_loop(1, n_blocks, body, 0)
    _interleave(advance(n_blocks - 1))


def _mix_c(pc, mu, w0, w_up, a0, a_up, g_up, k_k, k_a, r_k, ln_w, ln_b):
    b, s, _ = pc.shape
    g = GROUP_W
    n_pairs = N_HEADS // 2
    lora = jnp.zeros((LORA_W, 3 * g), F32)
    lora = lora.at[0:RANK_W, 0:g].set(w_up)
    lora = lora.at[RANK_W:RANK_W + RANK_A, g:2 * g].set(a_up)
    lora = lora.at[RANK_W + RANK_A:, 2 * g:].set(g_up)
    vec = lambda t: t.reshape(1, g)
    return pl.pallas_call(
        _rwkv_body,
        grid=(b,),
        in_specs=[pl.BlockSpec((1, s, C_COLS), lambda i: (i, 0, 0)), _resident((1, C_COLS)),
                  _resident((1, g)), _resident((LORA_W, 3 * g)), _resident((1, g)), _resident((1, g)),
                  _resident((1, g)), _resident((1, g)), _resident((1, g)), _resident((1, g))],
        out_specs=pl.BlockSpec((1, s, g), lambda i: (i, 0, 0)),
        out_shape=jax.ShapeDtypeStruct((b, s, g), BF16),
        scratch_shapes=[
            pltpu.VMEM((s // RWKV_CHUNK, n_pairs, RWKV_CHUNK + PAIR_W, PAIR_W), BF16),
            pltpu.VMEM((s // RWKV_CHUNK, n_pairs, RWKV_CHUNK + PAIR_W, PAIR_W), F32),
            pltpu.VMEM((s, g), F32),
            pltpu.VMEM((s, g), F32),
            pltpu.VMEM((s, g), F32),
            pltpu.VMEM((n_pairs, PAIR_W, PAIR_W), F32),
            pltpu.VMEM((1, C_COLS), F32),
        ],
        compiler_params=_params("parallel"),
        name="mix_c",
    )(pc, mu.reshape(1, C_COLS), vec(w0), lora.astype(BF16), vec(a0), vec(k_k), vec(k_a), vec(r_k),
      vec(ln_w), vec(ln_b))


def kernel(x, ffn1_pre_g, ffn1_w_gate, ffn1_w_up, ffn1_w_down, ffn1_post_g, mix_pre_g, w_in, sc_conv_w, sg_ln_w, sg_ln_b, sg_w, sg_b, rk_mu, rk_w0, rk_w_up, rk_a0, rk_a_up, rk_g_up, rk_k_k, rk_k_a, rk_r_k, rk_ln_w, rk_ln_b, cm_conv_w, cm_conv_b, cm_ln_w, cm_ln_b, w_out, mix_post_g, ffn2_pre_g, ffn2_w_gate, ffn2_w_up, ffn2_w_down, ffn2_post_g):
    b, s, d = x.shape
    t = b * s
    xf = x.reshape(t, d)
    for l in range(ffn1_pre_g.shape[0]):
        xf = _ffn(xf, l, ffn1_pre_g, ffn1_w_gate, ffn1_w_up, ffn1_w_down, ffn1_post_g)
        ya, yb, pc, yd = _mix_in(xf, s, l, mix_pre_g[l], w_in, sc_conv_w[l], sg_ln_w[l], sg_ln_b[l],
                                 sg_w[l], sg_b[l], cm_conv_w[l], cm_conv_b[l], cm_ln_w[l], cm_ln_b[l])
        yc = _mix_c(pc.reshape(b, s, C_COLS), rk_mu[l], rk_w0[l], rk_w_up[l], rk_a0[l], rk_a_up[l],
                    rk_g_up[l], rk_k_k[l], rk_k_a[l], rk_r_k[l].reshape(-1), rk_ln_w[l], rk_ln_b[l])
        xf = _ffn(xf, l, ffn2_pre_g, ffn2_w_gate, ffn2_w_up, ffn2_w_down, ffn2_post_g,
                  mix=((ya, yb, yc.reshape(t, GROUP_W), yd), w_out, mix_post_g))
    return xf.reshape(b, s, d)
```

```python
import functools
import math

import jax
import jax.numpy as jnp
from jax import lax
from jax.experimental import pallas as pl
from jax.experimental.pallas import tpu as pltpu

F32 = jnp.float32
BF16 = jnp.bfloat16

D_MODEL = 1024
D_FF = 2816
GROUP_W = 256
HEAD_DIM = 64
N_HEADS = 4
SC_WIDTH = 3
CHUNK = 128
CM_WIDTH = 31
RANK_W = 32
RANK_A = 32
RANK_G = 64
A_COLS = 3 * GROUP_W
B_COLS = 2 * GROUP_W
C_COLS = 3 * GROUP_W + RANK_W + RANK_A + RANK_G
D_COLS = 2 * GROUP_W
IN_COLS = A_COLS + B_COLS + C_COLS + D_COLS
LORA_W = RANK_W + RANK_A + RANK_G
RMS_EPS = 1e-6
LN_EPS = 1e-5
GN_EPS = 1e-5 * HEAD_DIM
FFN_RESID = 0.5

V7X_VMEM_BYTES = 64 * 1024 * 1024
VMEM_LIMIT_BYTES = V7X_VMEM_BYTES - 8 * 1024 * 1024
SUBLANES = 8

FFN_TILE = 512
SEQ_TILE = 1024
FFN_CHUNK = 256
N_FFN_CHUNKS = D_FF // FFN_CHUNK
RWKV_CHUNK = 64
RWKV_BLOCK = 4
CONV_PAD = 32
CONV_TILE = 128
PAIR_W = 2 * HEAD_DIM
assert CONV_PAD >= CM_WIDTH - 1 and CONV_PAD % SUBLANES == 0


def _params(*sem):
    return pltpu.CompilerParams(dimension_semantics=sem, vmem_limit_bytes=VMEM_LIMIT_BYTES)


def _resident(shape):
    nd = len(shape)
    return pl.BlockSpec(shape, lambda *_: (0,) * nd, pipeline_mode=pl.Buffered(1))


def _mm(a, b):
    return jnp.dot(a.astype(BF16), b.astype(BF16), preferred_element_type=F32)


def _mm_nt(a, b):
    return lax.dot_general(a.astype(BF16), b.astype(BF16), (((1,), (1,)), ((), ())),
                           preferred_element_type=F32)


def _mm_tn(a, b):
    return lax.dot_general(a.astype(BF16), b.astype(BF16), (((0,), (0,)), ((), ())),
                           preferred_element_type=F32)


def _interleave(*streams):
    live = list(streams)
    while live:
        live = [s for s in live if next(s, live) is not live]


def _sigmoid(x):
    return 0.5 * jnp.tanh(0.5 * x) + 0.5


def _rms(x, g):
    return x * lax.rsqrt(jnp.mean(x * x, axis=-1, keepdims=True) + RMS_EPS) * g


def _layer_norm(x, g, b, eps):
    mu = jnp.mean(x, axis=-1, keepdims=True)
    xc = x - mu
    var = jnp.mean(xc * xc, axis=-1, keepdims=True)
    return xc * lax.rsqrt(var + eps) * g + b


def _ffn_phases(step, x_fn, pre_g_ref, wg32_ref, wu32_ref, wd32_ref, post_g_ref, o_ref,
                wg_ref, wu_ref, wd_ref, acc_ref, res_ref, h_ref):
    def chunk(c, h):
        sl = slice(c * FFN_CHUNK, (c + 1) * FFN_CHUNK)
        g = jnp.dot(h, wg_ref[:, sl], preferred_element_type=F32)
        u = jnp.dot(h, wu_ref[:, sl], preferred_element_type=F32)
        act = (g * _sigmoid(g) * u).astype(BF16)
        part = jnp.dot(act, wd_ref[sl, :], preferred_element_type=F32)
        if c == 0:
            acc_ref[...] = part
        else:
            acc_ref[...] += part

    for c in range(N_FFN_CHUNKS):
        @pl.when(step == c)
        def _(c=c):
            sl = slice(c * FFN_CHUNK, (c + 1) * FFN_CHUNK)
            wg_ref[:, sl] = wg32_ref[0].astype(BF16)
            wu_ref[:, sl] = wu32_ref[0].astype(BF16)
            wd_ref[sl, :] = wd32_ref[0].astype(BF16)
            if c == 0:
                x = x_fn()
                res_ref[...] = x
                h_ref[...] = _rms(x, pre_g_ref[0]).astype(BF16)
            chunk(c, h_ref[...])
            if c + 1 == N_FFN_CHUNKS:
                o_ref[...] = res_ref[...] + FFN_RESID * _rms(acc_ref[...], post_g_ref[0])

    @pl.when(step >= N_FFN_CHUNKS)
    def _():
        x = x_fn()
        h = _rms(x, pre_g_ref[0]).astype(BF16)
        for c in range(N_FFN_CHUNKS):
            chunk(c, h)
        o_ref[...] = x + FFN_RESID * _rms(acc_ref[...], post_g_ref[0])


def _ffn_body(x_ref, *refs):
    _ffn_phases(pl.program_id(0), lambda: x_ref[...], *refs)


def _mix_out_ffn_body(x_ref, ya_ref, yb_ref, yc_ref, yd_ref, wo32_ref, mix_g_ref, *refs):
    wo_ref, ffn_refs = refs[-7], refs[:-7] + refs[-6:]
    step = pl.program_id(0)

    @pl.when(step == 0)
    def _():
        wo_ref[...] = wo32_ref[0].astype(BF16)

    def mixed():
        m = None
        for i, y_ref in enumerate((ya_ref, yb_ref, yc_ref, yd_ref)):
            part = jnp.dot(y_ref[...], wo_ref[i * GROUP_W:(i + 1) * GROUP_W, :], preferred_element_type=F32)
            m = part if m is None else m + part
        return x_ref[...] + _rms(m, mix_g_ref[0])

    _ffn_phases(step, mixed, *ffn_refs)


def _ffn(x, layer, pre_g, wg, wu, wd, post_g, mix=None):
    t, d = x.shape
    n_layers, _, f = wg.shape
    last = N_FFN_CHUNKS - 1
    tile = lambda width: pl.BlockSpec((FFN_TILE, width), lambda i: (jnp.maximum(i - last, 0), 0))
    gain = pl.BlockSpec((1, 1, d), lambda i: (layer, 0, 0))
    gains = lambda g: g.reshape(n_layers, 1, d)
    ffn_specs = [gain,
                 pl.BlockSpec((1, d, FFN_CHUNK), lambda i: (layer, 0, jnp.minimum(i, last))),
                 pl.BlockSpec((1, d, FFN_CHUNK), lambda i: (layer, 0, jnp.minimum(i, last))),
                 pl.BlockSpec((1, FFN_CHUNK, d), lambda i: (layer, jnp.minimum(i, last), 0)),
                 gain]
    ffn_args = (gains(pre_g), wg, wu, wd, gains(post_g))
    ffn_scratch = [pltpu.VMEM((d, f), BF16), pltpu.VMEM((d, f), BF16), pltpu.VMEM((f, d), BF16),
                   pltpu.VMEM((FFN_TILE, d), F32), pltpu.VMEM((FFN_TILE, d), F32),
                   pltpu.VMEM((FFN_TILE, d), BF16)]
    if mix is None:
        body, specs, args, scratch = _ffn_body, [tile(d)], (x,), ffn_scratch
    else:
        ys, w_out, mix_g = mix
        body = _mix_out_ffn_body
        specs = ([tile(d)] + [tile(GROUP_W)] * len(ys)
                 + [pl.BlockSpec((1, d, d), lambda i: (layer, 0, 0), pipeline_mode=pl.Buffered(1)), gain])
        args = (x, *ys, w_out, gains(mix_g))
        scratch = [pltpu.VMEM((d, d), BF16)] + ffn_scratch
    return pl.pallas_call(
        body,
        grid=(last + t // FFN_TILE,),
        in_specs=specs + ffn_specs,
        out_specs=tile(d),
        out_shape=jax.ShapeDtypeStruct((t, d), F32),
        scratch_shapes=scratch,
        compiler_params=_params("arbitrary"),
        name="ffn" if mix is None else "mix_out_ffn",
    )(*args, *ffn_args)


def _causal_conv_tile(zpad_ref, w_ref, width, t0):
    zh = zpad_ref[t0:t0 + CONV_TILE + CONV_PAD, :]
    acc = None
    for r in range(min(SUBLANES, width)):
        zr = zh if r == 0 else pltpu.roll(zh, r, 0)
        for q in range((width - 1 - r) // SUBLANES + 1):
            j = width - 1 - (SUBLANES * q + r)
            start = CONV_PAD - SUBLANES * q
            term = zr[start:start + CONV_TILE] * w_ref[j:j + 1, :]
            acc = term if acc is None else acc + term
    return acc


_OFF_B = A_COLS
_OFF_C = A_COLS + B_COLS
_OFF_D = A_COLS + B_COLS + C_COLS
IN_CHUNK = 384
N_IN_CHUNKS = IN_COLS // IN_CHUNK


def _mix_in_body(x_ref, g_ref, w32_ref, *refs, tiles_per_seq):
    w_ref = refs[-3]
    step = pl.program_id(0)
    for c in range(N_IN_CHUNKS):
        @pl.when(step == c)
        def _(c=c):
            w_ref[:, c * IN_CHUNK:(c + 1) * IN_CHUNK] = w32_ref[0].astype(BF16)

    @pl.when(step >= N_IN_CHUNKS)
    def _():
        first = (step - N_IN_CHUNKS) % tiles_per_seq == 0
        _mix_in_tile(first, x_ref, g_ref, w_ref, *refs[:-3], *refs[-2:])


def _mix_in_tile(first, x_ref, g_ref, w_ref, scw_ref, sg_lnw_ref, sg_lnb_ref, sgw_ref, sgb_ref,
                 cmw_ref, cmb_ref, cm_lnw_ref, cm_lnb_ref,
                 ya_ref, yb_ref, pc_ref, yd_ref, za_ref, zd_ref):
    g = GROUP_W
    ts = x_ref.shape[0]

    @pl.when(first)
    def _():
        za_ref[0:CONV_PAD, :] = jnp.zeros((CONV_PAD, g), F32)
        zd_ref[0:CONV_PAD, :] = jnp.zeros((CONV_PAD, g), F32)

    @pl.when(jnp.logical_not(first))
    def _():
        za_ref[0:CONV_PAD, :] = za_ref[ts:ts + CONV_PAD, :]
        zd_ref[0:CONV_PAD, :] = zd_ref[ts:ts + CONV_PAD, :]

    h = _rms(x_ref[...], g_ref[...]).astype(BF16)
    z1 = jnp.dot(h, w_ref[:, _OFF_D:_OFF_D + g], preferred_element_type=F32)
    z2 = jnp.dot(h, w_ref[:, _OFF_D + g:_OFF_D + 2 * g], preferred_element_type=F32)
    zd_ref[CONV_PAD:, :] = z1 * _sigmoid(z2)
    pa = jnp.dot(h, w_ref[:, 0:A_COLS], preferred_element_type=F32)
    za_ref[CONV_PAD:, :] = pa[:, g:2 * g] * pa[:, 2 * g:3 * g]
    pb = jnp.dot(h, w_ref[:, _OFF_B:_OFF_B + B_COLS], preferred_element_type=F32)
    pc_ref[...] = jnp.dot(h, w_ref[:, _OFF_C:_OFF_C + C_COLS], preferred_element_type=F32)

    for t0 in range(0, ts, CONV_TILE):
        rows = slice(t0, t0 + CONV_TILE)
        y = _causal_conv_tile(zd_ref, cmw_ref, CM_WIDTH, t0) + cmb_ref[...]
        y = _layer_norm(y, cm_lnw_ref[...], cm_lnb_ref[...], LN_EPS)
        yd_ref[rows, :] = (y * _sigmoid(y)).astype(BF16)
        ya_ref[rows, :] = (pa[rows, 0:g] * _causal_conv_tile(za_ref, scw_ref, SC_WIDTH, t0)).astype(BF16)

    row = lax.broadcasted_iota(jnp.int32, (CHUNK, CHUNK), 0)
    col = lax.broadcasted_iota(jnp.int32, (CHUNK, CHUNK), 1)
    w_tril = jnp.concatenate([jnp.where(row >= col, sgw_ref[hd], 0.0).astype(BF16) for hd in range(N_HEADS)],
                             axis=1)
    lane_head = lax.broadcasted_iota(jnp.int32, (CHUNK, g), 1) // HEAD_DIM
    for t0 in range(0, ts, CHUNK):
        rows = slice(t0, t0 + CHUNK)
        v = _layer_norm(pb[rows, g:2 * g], sg_lnw_ref[...], sg_lnb_ref[...], LN_EPS).astype(BF16)
        v_heads = jnp.concatenate([jnp.where(lane_head == hd, v, jnp.zeros_like(v)) for hd in range(N_HEADS)],
                                  axis=0)
        s = sgb_ref[...] + jnp.dot(w_tril, v_heads, preferred_element_type=F32)
        yb_ref[rows, :] = (pb[rows, 0:g] * s).astype(BF16)


def _mix_in(x, seq, layer, g, w_in, sc_conv_w, sg_ln_w, sg_ln_b, sg_w, sg_b, cm_conv_w, cm_conv_b, cm_ln_w,
            cm_ln_b):
    t, d = x.shape
    gw = GROUP_W
    vec = lambda v: v.reshape(1, gw)
    sg_bias = jnp.repeat(sg_b.T, HEAD_DIM, axis=1)
    tile = lambda width: pl.BlockSpec((SEQ_TILE, width), lambda i: (jnp.maximum(i - N_IN_CHUNKS, 0), 0))
    w_chunk = pl.BlockSpec((1, d, IN_CHUNK), lambda i: (layer, 0, jnp.minimum(i, N_IN_CHUNKS - 1)))
    return pl.pallas_call(
        functools.partial(_mix_in_body, tiles_per_seq=seq // SEQ_TILE),
        grid=(N_IN_CHUNKS + t // SEQ_TILE,),
        in_specs=[tile(d), _resident((1, d)), w_chunk, _resident((SC_WIDTH, gw)),
                  _resident((1, gw)), _resident((1, gw)), _resident((N_HEADS, CHUNK, CHUNK)),
                  _resident((CHUNK, gw)), _resident((CM_WIDTH, gw)), _resident((1, gw)),
                  _resident((1, gw)), _resident((1, gw))],
        out_specs=[tile(gw), tile(gw), tile(C_COLS), tile(gw)],
        out_shape=[jax.ShapeDtypeStruct((t, gw), BF16), jax.ShapeDtypeStruct((t, gw), BF16),
                   jax.ShapeDtypeStruct((t, C_COLS), F32), jax.ShapeDtypeStruct((t, gw), BF16)],
        scratch_shapes=[pltpu.VMEM((d, IN_COLS), BF16), pltpu.VMEM((SEQ_TILE + CONV_PAD, gw), F32),
                        pltpu.VMEM((SEQ_TILE + CONV_PAD, gw), F32)],
        compiler_params=_params("arbitrary"),
        name="mix_in",
    )(x, g.reshape(1, d), w_in, sc_conv_w, vec(sg_ln_w), vec(sg_ln_b), sg_w, sg_bias,
      cm_conv_w, vec(cm_conv_b), vec(cm_ln_w), vec(cm_ln_b))


def _split2(x):
    hi = x.astype(BF16)
    lo = (x - hi.astype(F32)).astype(BF16)
    return hi, lo


def _head_sum(x, ones_bd):
    hi, lo = _split2(x)
    return (jnp.dot(hi, ones_bd, preferred_element_type=F32)
            + jnp.dot(lo, ones_bd, preferred_element_type=F32))


def _rwkv_body(pc_ref, mu_ref, w0_ref, lora_ref, a0_ref, kk_ref, ka_ref, rk_ref, lnw_ref, lnb_ref,
               o_ref, gm_ref, yc_ref, rec_ref, bonus_ref, gate_ref, state_ref, prev_ref):
    g = GROUP_W
    lc = RWKV_CHUNK
    tb = RWKV_BLOCK * lc
    seq = pc_ref.shape[1]

    row = lax.broadcasted_iota(jnp.int32, (lc, 2 * lc), 0)
    col = lax.broadcasted_iota(jnp.int32, (lc, 2 * lc), 1) % lc
    strict2 = row > col
    incl2 = row >= col
    brow = lax.broadcasted_iota(jnp.int32, (tb, tb), 0)
    bcol = lax.broadcasted_iota(jnp.int32, (tb, tb), 1)
    tri_ones = jnp.where((brow // lc == bcol // lc) & (brow >= bcol), 1.0, 0.0).astype(BF16)
    bd_r = lax.broadcasted_iota(jnp.int32, (g, g), 0) // HEAD_DIM
    bd_c = lax.broadcasted_iota(jnp.int32, (g, g), 1) // HEAD_DIM
    ones_bd = jnp.where(bd_r == bd_c, 1.0, 0.0).astype(BF16)
    pr = lax.broadcasted_iota(jnp.int32, (PAIR_W, PAIR_W), 0)
    pcol = lax.broadcasted_iota(jnp.int32, (PAIR_W, PAIR_W), 1)
    pair_bd = (pr // HEAD_DIM) == (pcol // HEAD_DIM)
    pair_eye = pr == pcol
    first_head = lax.broadcasted_iota(jnp.int32, (lc, PAIR_W), 1) < HEAD_DIM
    first_head2 = jnp.concatenate([first_head, first_head], axis=1)
    row0 = lax.broadcasted_iota(jnp.int32, (tb, C_COLS), 0) == 0
    pad_v = jnp.zeros((lc, PAIR_W), BF16)

    def split(y):
        yb = y.astype(BF16)
        m = first_head if y.shape[1] == PAIR_W else first_head2
        zero = jnp.zeros_like(yb)
        return jnp.concatenate([jnp.where(m, yb, zero), jnp.where(m, zero, yb)], axis=0)

    state_ref[...] = jnp.zeros(state_ref.shape, F32)
    prev_ref[...] = jnp.zeros(prev_ref.shape, F32)

    def build(i):
        t0 = pl.multiple_of(i * tb, tb)
        x = pc_ref[0, pl.ds(t0, tb), :]
        xs = jnp.where(row0, prev_ref[...], pltpu.roll(x, 1, 0))
        prev_ref[...] = x[tb - 1:tb, :]
        xm = x + (xs - x) * mu_ref[...]
        r = xm[:, 0:g]
        k = xm[:, g:2 * g]
        v = xm[:, 2 * g:3 * g]
        lo_in = xm[:, 3 * g:3 * g + LORA_W]
        lane = lax.broadcasted_iota(jnp.int32, (tb, LORA_W), 1)
        lo_act = jnp.where(lane < RANK_W, jnp.tanh(lo_in),
                           jnp.where(lane < RANK_W + RANK_A, lo_in, _sigmoid(lo_in)))
        lora = _mm(lo_act, lora_ref[...])
        e = _sigmoid(w0_ref[...] + lora[:, 0:g]) * math.exp(-0.5)
        a_sig = _sigmoid(a0_ref[...] + lora[:, g:2 * g])
        gate_ref[pl.ds(t0, tb), :] = lora[:, 2 * g:3 * g]
        kk = k * kk_ref[...]
        kk = kk * jnp.minimum(lax.rsqrt(_head_sum(kk * kk, ones_bd)), 1e12)
        k = k * (1.0 + (a_sig - 1.0) * ka_ref[...])
        a_ = -kk
        b_ = kk * a_sig
        bonus_ref[pl.ds(t0, tb), :] = _head_sum(r * k * rk_ref[...], ones_bd) * v

        e_hi = e.astype(BF16)
        e_r = e - e_hi.astype(F32)
        e_mid = e_r.astype(BF16)
        e_lo = (e_r - e_mid.astype(F32)).astype(BF16)
        cs = (jnp.dot(tri_ones, e_hi, preferred_element_type=F32)
              + jnp.dot(tri_ones, e_mid, preferred_element_type=F32)
              + jnp.dot(tri_ones, e_lo, preferred_element_type=F32))
        cs_last = jnp.concatenate(
            [jnp.broadcast_to(cs[(ci + 1) * lc - 1:(ci + 1) * lc, :], (lc, g)) for ci in range(RWKV_BLOCK)],
            axis=0)
        w_inc = jnp.exp(-cs)
        w_exc = jnp.exp(e - cs)
        w_inv = jnp.exp(cs)
        w_fin = jnp.exp(cs - cs_last)
        w_chunk_all = jnp.exp(-cs_last)
        at_all = a_ * w_exc
        rt_all = r * w_inc
        bt_all = b_ * w_inv
        kt_all = k * w_inv
        bh_all = b_ * w_fin
        kh_all = k * w_fin
        yield

        probs = [(ci, p) for ci in range(RWKV_BLOCK) for p in range(N_HEADS // 2)]
        ops = []
        for ci, p in probs:
            rs = slice(ci * lc, (ci + 1) * lc)
            ps = slice(p * PAIR_W, (p + 1) * PAIR_W)
            ops.append(tuple(t[rs, ps] for t in (at_all, rt_all, bt_all, kt_all, bh_all, kh_all, v)))
        a_ab, a_ak, a_rb, a_rk = [], [], [], []
        for at, rt, bt, kt, bh, kh, vp in ops:
            lhs = jnp.concatenate([at, rt], axis=0)
            abk = _mm_nt(lhs, jnp.concatenate([split(bt), split(kt)], axis=0))
            ab, ak = abk[:, 0:2 * lc], abk[:, 2 * lc:]
            a_ab.append(jnp.where(strict2, ab[0:lc], 0.0))
            a_ak.append(jnp.where(strict2, ak[0:lc], 0.0))
            a_rb.append(jnp.where(incl2, ab[lc:], 0.0))
            a_rk.append(jnp.where(incl2, ak[lc:], 0.0))
        yield
        zs = [jnp.concatenate([op[0], _mm(a, split(op[6]))], axis=1) for a, op in zip(a_ak, ops)]
        yield
        aps = list(a_ab)
        n_fac = int(math.log2(lc))
        for f in range(n_fac):
            if f + 1 < n_fac:
                both = [_mm(ap, jnp.concatenate([split(z), split(ap)], axis=1)) for ap, z in zip(aps, zs)]
                zs = [z + b[:, 0:2 * PAIR_W] for z, b in zip(zs, both)]
                aps = [b[:, 2 * PAIR_W:] for b in both]
            else:
                zs = [z + _mm(ap, split(z)) for ap, z in zip(aps, zs)]
            yield
        for n, ((ci, p), (at, rt, bt, kt, bh, kh, vp)) in enumerate(zip(probs, ops)):
            ps = slice(p * PAIR_W, (p + 1) * PAIR_W)
            idx = i * RWKV_BLOCK + ci
            z = zs[n]
            zv = jnp.concatenate([z.astype(BF16), jnp.concatenate([pad_v, vp.astype(BF16)], axis=1)],
                                 axis=0)
            gy = _mm(jnp.concatenate([a_rb[n], a_rk[n]], axis=1),
                     jnp.concatenate([split(zv[0:lc]), split(zv[lc:])], axis=0))
            g_mat = rt + gy[:, 0:PAIR_W]
            y0 = gy[:, PAIR_W:]
            mc = _mm_tn(jnp.concatenate([bh, kh], axis=0), zv)
            w_chunk = w_chunk_all[ci * lc:ci * lc + 1, ps]
            m_mat = jnp.where(pair_bd, mc[:, 0:PAIR_W], 0.0) + jnp.where(pair_eye, w_chunk, 0.0)
            c_mat = jnp.where(pair_bd, mc[:, PAIR_W:], 0.0)
            gm_ref[idx, p, 0:lc, :] = g_mat.astype(BF16)
            gm_ref[idx, p, lc:, :] = m_mat.astype(BF16)
            yc_ref[idx, p, 0:lc, :] = y0
            yc_ref[idx, p, lc:, :] = c_mat

    def advance(i):
        for ci in range(RWKV_BLOCK):
            c = i * RWKV_BLOCK + ci
            t0 = pl.multiple_of(c * lc, lc)
            for p in range(N_HEADS // 2):
                step = jnp.dot(gm_ref[c, p], state_ref[p].astype(BF16), preferred_element_type=F32)
                step = step + yc_ref[c, p]
                rec_ref[pl.ds(t0, lc), p * PAIR_W:(p + 1) * PAIR_W] = step[0:lc]
                state_ref[p] = step[lc:]
            yield
        t0 = pl.multiple_of(i * tb, tb)
        o = rec_ref[pl.ds(t0, tb), :]
        mean = _head_sum(o, ones_bd) * (1.0 / HEAD_DIM)
        yield
        oc = o - mean
        var = _head_sum(oc * oc, ones_bd) * (1.0 / HEAD_DIM)
        yield
        o = oc * lax.rsqrt(var + GN_EPS) * lnw_ref[...] + lnb_ref[...]
        o = (o + bonus_ref[pl.ds(t0, tb), :]) * gate_ref[pl.ds(t0, tb), :]
        o_ref[0, pl.ds(t0, tb), :] = o.astype(BF16)

    n_blocks = seq // tb
    _interleave(build(0))

    def body(i, carry):
        _interleave(build(i), advance(i - 1))
        return carry

    lax.fori_loop(1, n_blocks, body, 0)
    _interleave(advance(n_blocks - 1))


def _mix_c(pc, mu, w0, w_up, a0, a_up, g_up, k_k, k_a, r_k, ln_w, ln_b):
    b, s, _ = pc.shape
    g = GROUP_W
    n_pairs = N_HEADS // 2
    lora = jnp.zeros((LORA_W, 3 * g), F32)
    lora = lora.at[0:RANK_W, 0:g].set(w_up)
    lora = lora.at[RANK_W:RANK_W + RANK_A, g:2 * g].set(a_up)
    lora = lora.at[RANK_W + RANK_A:, 2 * g:].set(g_up)
    vec = lambda t: t.reshape(1, g)
    return pl.pallas_call(
        _rwkv_body,
        grid=(b,),
        in_specs=[pl.BlockSpec((1, s, C_COLS), lambda i: (i, 0, 0)), _resident((1, C_COLS)),
                  _resident((1, g)), _resident((LORA_W, 3 * g)), _resident((1, g)), _resident((1, g)),
                  _resident((1, g)), _resident((1, g)), _resident((1, g)), _resident((1, g))],
        out_specs=pl.BlockSpec((1, s, g), lambda i: (i, 0, 0)),
        out_shape=jax.ShapeDtypeStruct((b, s, g), BF16),
        scratch_shapes=[
            pltpu.VMEM((s // RWKV_CHUNK, n_pairs, RWKV_CHUNK + PAIR_W, PAIR_W), BF16),
            pltpu.VMEM((s // RWKV_CHUNK, n_pairs, RWKV_CHUNK + PAIR_W, PAIR_W), F32),
            pltpu.VMEM((s, g), F32),
            pltpu.VMEM((s, g), F32),
            pltpu.VMEM((s, g), F32),
            pltpu.VMEM((n_pairs, PAIR_W, PAIR_W), F32),
            pltpu.VMEM((1, C_COLS), F32),
        ],
        compiler_params=_params("parallel"),
        name="mix_c",
    )(pc, mu.reshape(1, C_COLS), vec(w0), lora.astype(BF16), vec(a0), vec(k_k), vec(k_a), vec(r_k),
      vec(ln_w), vec(ln_b))


def kernel(x, ffn1_pre_g, ffn1_w_gate, ffn1_w_up, ffn1_w_down, ffn1_post_g, mix_pre_g, w_in, sc_conv_w, sg_ln_w, sg_ln_b, sg_w, sg_b, rk_mu, rk_w0, rk_w_up, rk_a0, rk_a_up, rk_g_up, rk_k_k, rk_k_a, rk_r_k, rk_ln_w, rk_ln_b, cm_conv_w, cm_conv_b, cm_ln_w, cm_ln_b, w_out, mix_post_g, ffn2_pre_g, ffn2_w_gate, ffn2_w_up, ffn2_w_down, ffn2_post_g):
    b, s, d = x.shape
    t = b * s
    xf = x.reshape(t, d)
    for l in range(ffn1_pre_g.shape[0]):
        xf = _ffn(xf, l, ffn1_pre_g, ffn1_w_gate, ffn1_w_up, ffn1_w_down, ffn1_post_g)
        ya, yb, pc, yd = _mix_in(xf, s, l, mix_pre_g[l], w_in, sc_conv_w[l], sg_ln_w[l], sg_ln_b[l],
                                 sg_w[l], sg_b[l], cm_conv_w[l], cm_conv_b[l], cm_ln_w[l], cm_ln_b[l])
        yc = _mix_c(pc.reshape(b, s, C_COLS), rk_mu[l], rk_w0[l], rk_w_up[l], rk_a0[l], rk_a_up[l],
                    rk_g_up[l], rk_k_k[l], rk_k_a[l], rk_r_k[l].reshape(-1), rk_ln_w[l], rk_ln_b[l])
        xf = _ffn(xf, l, ffn2_pre_g, ffn2_w_gate, ffn2_w_up, ffn2_w_down, ffn2_post_g,
                  mix=((ya, yb, yc.reshape(t, GROUP_W), yd), w_out, mix_post_g))
    return xf.reshape(b, s, d)
```

```python
import functools
import math

import jax
import jax.numpy as jnp
from jax import lax
from jax.experimental import pallas as pl
from jax.experimental.pallas import tpu as pltpu

F32 = jnp.float32
BF16 = jnp.bfloat16

D_MODEL = 1024
D_FF = 2816
GROUP_W = 256
HEAD_DIM = 64
N_HEADS = 4
SC_WIDTH = 3
CHUNK = 128
CM_WIDTH = 31
RANK_W = 32
RANK_A = 32
RANK_G = 64
A_COLS = 3 * GROUP_W
B_COLS = 2 * GROUP_W
C_COLS = 3 * GROUP_W + RANK_W + RANK_A + RANK_G
D_COLS = 2 * GROUP_W
IN_COLS = A_COLS + B_COLS + C_COLS + D_COLS
LORA_W = RANK_W + RANK_A + RANK_G
RMS_EPS = 1e-6
LN_EPS = 1e-5
GN_EPS = 1e-5 * HEAD_DIM
FFN_RESID = 0.5

V7X_VMEM_BYTES = 64 * 1024 * 1024
VMEM_LIMIT_BYTES = V7X_VMEM_BYTES - 8 * 1024 * 1024
SUBLANES = 8

FFN_TILE = 512
SEQ_TILE = 1024
FFN_CHUNK = 256
N_FFN_CHUNKS = D_FF // FFN_CHUNK
RWKV_CHUNK = 64
RWKV_BLOCK = 4
CONV_PAD = 32
CONV_TILE = 128
PAIR_W = 2 * HEAD_DIM
assert CONV_PAD >= CM_WIDTH - 1 and CONV_PAD % SUBLANES == 0


def _params(*sem):
    return pltpu.CompilerParams(dimension_semantics=sem, vmem_limit_bytes=VMEM_LIMIT_BYTES)


def _resident(shape):
    nd = len(shape)
    return pl.BlockSpec(shape, lambda *_: (0,) * nd, pipeline_mode=pl.Buffered(1))


def _mm(a, b):
    return jnp.dot(a.astype(BF16), b.astype(BF16), preferred_element_type=F32)


def _mm_nt(a, b):
    return lax.dot_general(a.astype(BF16), b.astype(BF16), (((1,), (1,)), ((), ())),
                           preferred_element_type=F32)


def _mm_tn(a, b):
    return lax.dot_general(a.astype(BF16), b.astype(BF16), (((0,), (0,)), ((), ())),
                           preferred_element_type=F32)


def _interleave(*streams):
    live = list(streams)
    while live:
        live = [s for s in live if next(s, live) is not live]


def _sigmoid(x):
    return 0.5 * jnp.tanh(0.5 * x) + 0.5


def _rms(x, g):
    return x * lax.rsqrt(jnp.mean(x * x, axis=-1, keepdims=True) + RMS_EPS) * g


def _layer_norm(x, g, b, eps):
    mu = jnp.mean(x, axis=-1, keepdims=True)
    xc = x - mu
    var = jnp.mean(xc * xc, axis=-1, keepdims=True)
    return xc * lax.rsqrt(var + eps) * g + b


def _ffn_tail(x, pre_g_ref, wg_ref, wu_ref, wd_ref, post_g_ref, o_ref, acc_ref):
    h = _rms(x, pre_g_ref[...]).astype(BF16)
    for c in range(D_FF // FFN_CHUNK):
        sl = slice(c * FFN_CHUNK, (c + 1) * FFN_CHUNK)
        g = jnp.dot(h, wg_ref[:, sl], preferred_element_type=F32)
        u = jnp.dot(h, wu_ref[:, sl], preferred_element_type=F32)
        act = (g * _sigmoid(g) * u).astype(BF16)
        part = jnp.dot(act, wd_ref[sl, :], preferred_element_type=F32)
        if c == 0:
            acc_ref[...] = part
        else:
            acc_ref[...] += part
    o_ref[...] = x + FFN_RESID * _rms(acc_ref[...], post_g_ref[...])


def _ffn_phases(step, x_fn, pre_g_ref, wg32_ref, wu32_ref, wd32_ref, post_g_ref, o_ref,
                wg_ref, wu_ref, wd_ref, acc_ref):
    for c in range(N_FFN_CHUNKS):
        @pl.when(step == c)
        def _(c=c):
            sl = slice(c * FFN_CHUNK, (c + 1) * FFN_CHUNK)
            wg_ref[:, sl] = wg32_ref[0].astype(BF16)
            wu_ref[:, sl] = wu32_ref[0].astype(BF16)
            wd_ref[sl, :] = wd32_ref[0].astype(BF16)

    @pl.when(step >= N_FFN_CHUNKS)
    def _():
        _ffn_tail(x_fn(), pre_g_ref.at[0], wg_ref, wu_ref, wd_ref, post_g_ref.at[0], o_ref, acc_ref)


def _ffn_body(x_ref, *refs):
    _ffn_phases(pl.program_id(0), lambda: x_ref[...], *refs)


def _mix_out_ffn_body(x_ref, ya_ref, yb_ref, yc_ref, yd_ref, wo32_ref, mix_g_ref, *refs):
    wo_ref, ffn_refs = refs[-5], refs[:-5] + refs[-4:]
    step = pl.program_id(0)

    @pl.when(step == 0)
    def _():
        wo_ref[...] = wo32_ref[0].astype(BF16)

    def mixed():
        m = None
        for i, y_ref in enumerate((ya_ref, yb_ref, yc_ref, yd_ref)):
            part = jnp.dot(y_ref[...], wo_ref[i * GROUP_W:(i + 1) * GROUP_W, :], preferred_element_type=F32)
            m = part if m is None else m + part
        return x_ref[...] + _rms(m, mix_g_ref[0])

    _ffn_phases(step, mixed, *ffn_refs)


def _ffn(x, layer, pre_g, wg, wu, wd, post_g, mix=None):
    t, d = x.shape
    n_layers, _, f = wg.shape
    last = N_FFN_CHUNKS - 1
    tile = lambda width: pl.BlockSpec((FFN_TILE, width), lambda i: (jnp.maximum(i - N_FFN_CHUNKS, 0), 0))
    gain = pl.BlockSpec((1, 1, d), lambda i: (layer, 0, 0))
    gains = lambda g: g.reshape(n_layers, 1, d)
    ffn_specs = [gain,
                 pl.BlockSpec((1, d, FFN_CHUNK), lambda i: (layer, 0, jnp.minimum(i, last))),
                 pl.BlockSpec((1, d, FFN_CHUNK), lambda i: (layer, 0, jnp.minimum(i, last))),
                 pl.BlockSpec((1, FFN_CHUNK, d), lambda i: (layer, jnp.minimum(i, last), 0)),
                 gain]
    ffn_args = (gains(pre_g), wg, wu, wd, gains(post_g))
    ffn_scratch = [pltpu.VMEM((d, f), BF16), pltpu.VMEM((d, f), BF16), pltpu.VMEM((f, d), BF16),
                   pltpu.VMEM((FFN_TILE, d), F32)]
    if mix is None:
        body, specs, args, scratch = _ffn_body, [tile(d)], (x,), ffn_scratch
    else:
        ys, w_out, mix_g = mix
        body = _mix_out_ffn_body
        specs = ([tile(d)] + [tile(GROUP_W)] * len(ys)
                 + [pl.BlockSpec((1, d, d), lambda i: (layer, 0, 0), pipeline_mode=pl.Buffered(1)), gain])
        args = (x, *ys, w_out, gains(mix_g))
        scratch = [pltpu.VMEM((d, d), BF16)] + ffn_scratch
    return pl.pallas_call(
        body,
        grid=(N_FFN_CHUNKS + t // FFN_TILE,),
        in_specs=specs + ffn_specs,
        out_specs=tile(d),
        out_shape=jax.ShapeDtypeStruct((t, d), F32),
        scratch_shapes=scratch,
        compiler_params=_params("arbitrary"),
        name="ffn" if mix is None else "mix_out_ffn",
    )(*args, *ffn_args)


def _causal_conv_tile(zpad_ref, w_ref, width, t0):
    zh = zpad_ref[t0:t0 + CONV_TILE + CONV_PAD, :]
    acc = None
    for r in range(min(SUBLANES, width)):
        zr = zh if r == 0 else pltpu.roll(zh, r, 0)
        for q in range((width - 1 - r) // SUBLANES + 1):
            j = width - 1 - (SUBLANES * q + r)
            start = CONV_PAD - SUBLANES * q
            term = zr[start:start + CONV_TILE] * w_ref[j:j + 1, :]
            acc = term if acc is None else acc + term
    return acc


_OFF_B = A_COLS
_OFF_C = A_COLS + B_COLS
_OFF_D = A_COLS + B_COLS + C_COLS
IN_CHUNK = 384
N_IN_CHUNKS = IN_COLS // IN_CHUNK


def _mix_in_body(x_ref, g_ref, w32_ref, *refs, tiles_per_seq):
    w_ref = refs[-3]
    step = pl.program_id(0)
    for c in range(N_IN_CHUNKS):
        @pl.when(step == c)
        def _(c=c):
            w_ref[:, c * IN_CHUNK:(c + 1) * IN_CHUNK] = w32_ref[0].astype(BF16)

    @pl.when(step >= N_IN_CHUNKS)
    def _():
        first = (step - N_IN_CHUNKS) % tiles_per_seq == 0
        _mix_in_tile(first, x_ref, g_ref, w_ref, *refs[:-3], *refs[-2:])


def _mix_in_tile(first, x_ref, g_ref, w_ref, scw_ref, sg_lnw_ref, sg_lnb_ref, sgw_ref, sgb_ref,
                 cmw_ref, cmb_ref, cm_lnw_ref, cm_lnb_ref,
                 ya_ref, yb_ref, pc_ref, yd_ref, za_ref, zd_ref):
    g = GROUP_W
    ts = x_ref.shape[0]

    @pl.when(first)
    def _():
        za_ref[0:CONV_PAD, :] = jnp.zeros((CONV_PAD, g), F32)
        zd_ref[0:CONV_PAD, :] = jnp.zeros((CONV_PAD, g), F32)

    @pl.when(jnp.logical_not(first))
    def _():
        za_ref[0:CONV_PAD, :] = za_ref[ts:ts + CONV_PAD, :]
        zd_ref[0:CONV_PAD, :] = zd_ref[ts:ts + CONV_PAD, :]

    h = _rms(x_ref[...], g_ref[...]).astype(BF16)
    z1 = jnp.dot(h, w_ref[:, _OFF_D:_OFF_D + g], preferred_element_type=F32)
    z2 = jnp.dot(h, w_ref[:, _OFF_D + g:_OFF_D + 2 * g], preferred_element_type=F32)
    zd_ref[CONV_PAD:, :] = z1 * _sigmoid(z2)
    pa = jnp.dot(h, w_ref[:, 0:A_COLS], preferred_element_type=F32)
    za_ref[CONV_PAD:, :] = pa[:, g:2 * g] * pa[:, 2 * g:3 * g]
    pb = jnp.dot(h, w_ref[:, _OFF_B:_OFF_B + B_COLS], preferred_element_type=F32)
    pc_ref[...] = jnp.dot(h, w_ref[:, _OFF_C:_OFF_C + C_COLS], preferred_element_type=F32)

    for t0 in range(0, ts, CONV_TILE):
        rows = slice(t0, t0 + CONV_TILE)
        y = _causal_conv_tile(zd_ref, cmw_ref, CM_WIDTH, t0) + cmb_ref[...]
        y = _layer_norm(y, cm_lnw_ref[...], cm_lnb_ref[...], LN_EPS)
        yd_ref[rows, :] = (y * _sigmoid(y)).astype(BF16)
        ya_ref[rows, :] = (pa[rows, 0:g] * _causal_conv_tile(za_ref, scw_ref, SC_WIDTH, t0)).astype(BF16)

    row = lax.broadcasted_iota(jnp.int32, (CHUNK, CHUNK), 0)
    col = lax.broadcasted_iota(jnp.int32, (CHUNK, CHUNK), 1)
    w_tril = jnp.concatenate([jnp.where(row >= col, sgw_ref[hd], 0.0).astype(BF16) for hd in range(N_HEADS)],
                             axis=1)
    lane_head = lax.broadcasted_iota(jnp.int32, (CHUNK, g), 1) // HEAD_DIM
    for t0 in range(0, ts, CHUNK):
        rows = slice(t0, t0 + CHUNK)
        v = _layer_norm(pb[rows, g:2 * g], sg_lnw_ref[...], sg_lnb_ref[...], LN_EPS).astype(BF16)
        v_heads = jnp.concatenate([jnp.where(lane_head == hd, v, jnp.zeros_like(v)) for hd in range(N_HEADS)],
                                  axis=0)
        s = sgb_ref[...] + jnp.dot(w_tril, v_heads, preferred_element_type=F32)
        yb_ref[rows, :] = (pb[rows, 0:g] * s).astype(BF16)


def _mix_in(x, seq, layer, g, w_in, sc_conv_w, sg_ln_w, sg_ln_b, sg_w, sg_b, cm_conv_w, cm_conv_b, cm_ln_w,
            cm_ln_b):
    t, d = x.shape
    gw = GROUP_W
    vec = lambda v: v.reshape(1, gw)
    sg_bias = jnp.repeat(sg_b.T, HEAD_DIM, axis=1)
    tile = lambda width: pl.BlockSpec((SEQ_TILE, width), lambda i: (jnp.maximum(i - N_IN_CHUNKS, 0), 0))
    w_chunk = pl.BlockSpec((1, d, IN_CHUNK), lambda i: (layer, 0, jnp.minimum(i, N_IN_CHUNKS - 1)))
    return pl.pallas_call(
        functools.partial(_mix_in_body, tiles_per_seq=seq // SEQ_TILE),
        grid=(N_IN_CHUNKS + t // SEQ_TILE,),
        in_specs=[tile(d), _resident((1, d)), w_chunk, _resident((SC_WIDTH, gw)),
                  _resident((1, gw)), _resident((1, gw)), _resident((N_HEADS, CHUNK, CHUNK)),
                  _resident((CHUNK, gw)), _resident((CM_WIDTH, gw)), _resident((1, gw)),
                  _resident((1, gw)), _resident((1, gw))],
        out_specs=[tile(gw), tile(gw), tile(C_COLS), tile(gw)],
        out_shape=[jax.ShapeDtypeStruct((t, gw), BF16), jax.ShapeDtypeStruct((t, gw), BF16),
                   jax.ShapeDtypeStruct((t, C_COLS), F32), jax.ShapeDtypeStruct((t, gw), BF16)],
        scratch_shapes=[pltpu.VMEM((d, IN_COLS), BF16), pltpu.VMEM((SEQ_TILE + CONV_PAD, gw), F32),
                        pltpu.VMEM((SEQ_TILE + CONV_PAD, gw), F32)],
        compiler_params=_params("arbitrary"),
        name="mix_in",
    )(x, g.reshape(1, d), w_in, sc_conv_w, vec(sg_ln_w), vec(sg_ln_b), sg_w, sg_bias,
      cm_conv_w, vec(cm_conv_b), vec(cm_ln_w), vec(cm_ln_b))


def _split2(x):
    hi = x.astype(BF16)
    lo = (x - hi.astype(F32)).astype(BF16)
    return hi, lo


def _head_sum(x, ones_bd):
    hi, lo = _split2(x)
    return (jnp.dot(hi, ones_bd, preferred_element_type=F32)
            + jnp.dot(lo, ones_bd, preferred_element_type=F32))


def _rwkv_body(pc_ref, mu_ref, w0_ref, lora_ref, a0_ref, kk_ref, ka_ref, rk_ref, lnw_ref, lnb_ref,
               o_ref, gm_ref, yc_ref, rec_ref, bonus_ref, gate_ref, state_ref, prev_ref):
    g = GROUP_W
    lc = RWKV_CHUNK
    tb = RWKV_BLOCK * lc
    seq = pc_ref.shape[1]

    row = lax.broadcasted_iota(jnp.int32, (lc, 2 * lc), 0)
    col = lax.broadcasted_iota(jnp.int32, (lc, 2 * lc), 1) % lc
    strict2 = row > col
    incl2 = row >= col
    brow = lax.broadcasted_iota(jnp.int32, (tb, tb), 0)
    bcol = lax.broadcasted_iota(jnp.int32, (tb, tb), 1)
    tri_ones = jnp.where((brow // lc == bcol // lc) & (brow >= bcol), 1.0, 0.0).astype(BF16)
    bd_r = lax.broadcasted_iota(jnp.int32, (g, g), 0) // HEAD_DIM
    bd_c = lax.broadcasted_iota(jnp.int32, (g, g), 1) // HEAD_DIM
    ones_bd = jnp.where(bd_r == bd_c, 1.0, 0.0).astype(BF16)
    pr = lax.broadcasted_iota(jnp.int32, (PAIR_W, PAIR_W), 0)
    pcol = lax.broadcasted_iota(jnp.int32, (PAIR_W, PAIR_W), 1)
    pair_bd = (pr // HEAD_DIM) == (pcol // HEAD_DIM)
    pair_eye = pr == pcol
    first_head = lax.broadcasted_iota(jnp.int32, (lc, PAIR_W), 1) < HEAD_DIM
    first_head2 = jnp.concatenate([first_head, first_head], axis=1)
    row0 = lax.broadcasted_iota(jnp.int32, (tb, C_COLS), 0) == 0
    pad_v = jnp.zeros((lc, PAIR_W), BF16)

    def split(y):
        yb = y.astype(BF16)
        m = first_head if y.shape[1] == PAIR_W else first_head2
        zero = jnp.zeros_like(yb)
        return jnp.concatenate([jnp.where(m, yb, zero), jnp.where(m, zero, yb)], axis=0)

    state_ref[...] = jnp.zeros(state_ref.shape, F32)
    prev_ref[...] = jnp.zeros(prev_ref.shape, F32)

    def build(i):
        t0 = pl.multiple_of(i * tb, tb)
        x = pc_ref[0, pl.ds(t0, tb), :]
        xs = jnp.where(row0, prev_ref[...], pltpu.roll(x, 1, 0))
        prev_ref[...] = x[tb - 1:tb, :]
        xm = x + (xs - x) * mu_ref[...]
        r = xm[:, 0:g]
        k = xm[:, g:2 * g]
        v = xm[:, 2 * g:3 * g]
        lo_in = xm[:, 3 * g:3 * g + LORA_W]
        lane = lax.broadcasted_iota(jnp.int32, (tb, LORA_W), 1)
        lo_act = jnp.where(lane < RANK_W, jnp.tanh(lo_in),
                           jnp.where(lane < RANK_W + RANK_A, lo_in, _sigmoid(lo_in)))
        lora = _mm(lo_act, lora_ref[...])
        e = _sigmoid(w0_ref[...] + lora[:, 0:g]) * math.exp(-0.5)
        a_sig = _sigmoid(a0_ref[...] + lora[:, g:2 * g])
        gate_ref[pl.ds(t0, tb), :] = lora[:, 2 * g:3 * g]
        kk = k * kk_ref[...]
        kk = kk * jnp.minimum(lax.rsqrt(_head_sum(kk * kk, ones_bd)), 1e12)
        k = k * (1.0 + (a_sig - 1.0) * ka_ref[...])
        a_ = -kk
        b_ = kk * a_sig
        bonus_ref[pl.ds(t0, tb), :] = _head_sum(r * k * rk_ref[...], ones_bd) * v

        e_hi = e.astype(BF16)
        e_r = e - e_hi.astype(F32)
        e_mid = e_r.astype(BF16)
        e_lo = (e_r - e_mid.astype(F32)).astype(BF16)
        cs = (jnp.dot(tri_ones, e_hi, preferred_element_type=F32)
              + jnp.dot(tri_ones, e_mid, preferred_element_type=F32)
              + jnp.dot(tri_ones, e_lo, preferred_element_type=F32))
        w_chunk_all = jnp.concatenate(
            [jnp.broadcast_to(jnp.exp(-cs[(ci + 1) * lc - 1:(ci + 1) * lc, :]), (lc, g))
             for ci in range(RWKV_BLOCK)], axis=0)
        w_inc = jnp.exp(-cs)
        w_exc = jnp.exp(e - cs)
        w_inv = jnp.exp(cs)
        w_fin = w_inv * w_chunk_all
        at_all = a_ * w_exc
        rt_all = r * w_inc
        bt_all = b_ * w_inv
        kt_all = k * w_inv
        bh_all = b_ * w_fin
        kh_all = k * w_fin
        yield

        probs = [(ci, p) for ci in range(RWKV_BLOCK) for p in range(N_HEADS // 2)]
        ops = []
        for ci, p in probs:
            rs = slice(ci * lc, (ci + 1) * lc)
            ps = slice(p * PAIR_W, (p + 1) * PAIR_W)
            ops.append(tuple(t[rs, ps] for t in (at_all, rt_all, bt_all, kt_all, bh_all, kh_all, v)))
        a_ab, a_ak, a_rb, a_rk = [], [], [], []
        for at, rt, bt, kt, bh, kh, vp in ops:
            lhs = jnp.concatenate([at, rt], axis=0)
            abk = _mm_nt(lhs, jnp.concatenate([split(bt), split(kt)], axis=0))
            ab, ak = abk[:, 0:2 * lc], abk[:, 2 * lc:]
            a_ab.append(jnp.where(strict2, ab[0:lc], 0.0))
            a_ak.append(jnp.where(strict2, ak[0:lc], 0.0))
            a_rb.append(jnp.where(incl2, ab[lc:], 0.0))
            a_rk.append(jnp.where(incl2, ak[lc:], 0.0))
        yield
        zs = [jnp.concatenate([op[0], _mm(a, split(op[6]))], axis=1) for a, op in zip(a_ak, ops)]
        yield
        aps = list(a_ab)
        n_fac = int(math.log2(lc))
        for f in range(n_fac):
            if f + 1 < n_fac:
                both = [_mm(ap, jnp.concatenate([split(z), split(ap)], axis=1)) for ap, z in zip(aps, zs)]
                zs = [z + b[:, 0:2 * PAIR_W] for z, b in zip(zs, both)]
                aps = [b[:, 2 * PAIR_W:] for b in both]
            else:
                zs = [z + _mm(ap, split(z)) for ap, z in zip(aps, zs)]
            yield
        for n, ((ci, p), (at, rt, bt, kt, bh, kh, vp)) in enumerate(zip(probs, ops)):
            ps = slice(p * PAIR_W, (p + 1) * PAIR_W)
            idx = i * RWKV_BLOCK + ci
            z = zs[n]
            zv = jnp.concatenate([z.astype(BF16), jnp.concatenate([pad_v, vp.astype(BF16)], axis=1)],
                                 axis=0)
            gy = _mm(jnp.concatenate([a_rb[n], a_rk[n]], axis=1),
                     jnp.concatenate([split(zv[0:lc]), split(zv[lc:])], axis=0))
            g_mat = rt + gy[:, 0:PAIR_W]
            y0 = gy[:, PAIR_W:]
            mc = _mm_tn(jnp.concatenate([bh, kh], axis=0), zv)
            w_chunk = w_chunk_all[ci * lc:ci * lc + 1, ps]
            m_mat = jnp.where(pair_bd, mc[:, 0:PAIR_W], 0.0) + jnp.where(pair_eye, w_chunk, 0.0)
            c_mat = jnp.where(pair_bd, mc[:, PAIR_W:], 0.0)
            gm_ref[idx, p, 0:lc, :] = g_mat.astype(BF16)
            gm_ref[idx, p, lc:, :] = m_mat.astype(BF16)
            yc_ref[idx, p, 0:lc, :] = y0
            yc_ref[idx, p, lc:, :] = c_mat

    def advance(i):
        for ci in range(RWKV_BLOCK):
            c = i * RWKV_BLOCK + ci
            t0 = pl.multiple_of(c * lc, lc)
            for p in range(N_HEADS // 2):
                step = jnp.dot(gm_ref[c, p], state_ref[p].astype(BF16), preferred_element_type=F32)
                step = step + yc_ref[c, p]
                rec_ref[pl.ds(t0, lc), p * PAIR_W:(p + 1) * PAIR_W] = step[0:lc]
                state_ref[p] = step[lc:]
            yield
        t0 = pl.multiple_of(i * tb, tb)
        o = rec_ref[pl.ds(t0, tb), :]
        mean = _head_sum(o, ones_bd) * (1.0 / HEAD_DIM)
        yield
        oc = o - mean
        var = _head_sum(oc * oc, ones_bd) * (1.0 / HEAD_DIM)
        yield
        o = oc * lax.rsqrt(var + GN_EPS) * lnw_ref[...] + lnb_ref[...]
        o = (o + bonus_ref[pl.ds(t0, tb), :]) * gate_ref[pl.ds(t0, tb), :]
        o_ref[0, pl.ds(t0, tb), :] = o.astype(BF16)

    n_blocks = seq // tb
    _interleave(build(0))

    def body(i, carry):
        _interleave(build(i), advance(i - 1))
        return carry

    lax.fori_loop(1, n_blocks, body, 0)
    _interleave(advance(n_blocks - 1))


def _mix_c(pc, mu, w0, w_up, a0, a_up, g_up, k_k, k_a, r_k, ln_w, ln_b):
    b, s, _ = pc.shape
    g = GROUP_W
    n_pairs = N_HEADS // 2
    lora = jnp.zeros((LORA_W, 3 * g), F32)
    lora = lora.at[0:RANK_W, 0:g].set(w_up)
    lora = lora.at[RANK_W:RANK_W + RANK_A, g:2 * g].set(a_up)
    lora = lora.at[RANK_W + RANK_A:, 2 * g:].set(g_up)
    vec = lambda t: t.reshape(1, g)
    return pl.pallas_call(
        _rwkv_body,
        grid=(b,),
        in_specs=[pl.BlockSpec((1, s, C_COLS), lambda i: (i, 0, 0)), _resident((1, C_COLS)),
                  _resident((1, g)), _resident((LORA_W, 3 * g)), _resident((1, g)), _resident((1, g)),
                  _resident((1, g)), _resident((1, g)), _resident((1, g)), _resident((1, g))],
        out_specs=pl.BlockSpec((1, s, g), lambda i: (i, 0, 0)),
        out_shape=jax.ShapeDtypeStruct((b, s, g), BF16),
        scratch_shapes=[
            pltpu.VMEM((s // RWKV_CHUNK, n_pairs, RWKV_CHUNK + PAIR_W, PAIR_W), BF16),
            pltpu.VMEM((s // RWKV_CHUNK, n_pairs, RWKV_CHUNK + PAIR_W, PAIR_W), F32),
            pltpu.VMEM((s, g), F32),
            pltpu.VMEM((s, g), F32),
            pltpu.VMEM((s, g), F32),
            pltpu.VMEM((n_pairs, PAIR_W, PAIR_W), F32),
            pltpu.VMEM((1, C_COLS), F32),
        ],
        compiler_params=_params("parallel"),
        name="mix_c",
    )(pc, mu.reshape(1, C_COLS), vec(w0), lora.astype(BF16), vec(a0), vec(k_k), vec(k_a), vec(r_k),
      vec(ln_w), vec(ln_b))


def kernel(x, ffn1_pre_g, ffn1_w_gate, ffn1_w_up, ffn1_w_down, ffn1_post_g, mix_pre_g, w_in, sc_conv_w, sg_ln_w, sg_ln_b, sg_w, sg_b, rk_mu, rk_w0, rk_w_up, rk_a0, rk_a_up, rk_g_up, rk_k_k, rk_k_a, rk_r_k, rk_ln_w, rk_ln_b, cm_conv_w, cm_conv_b, cm_ln_w, cm_ln_b, w_out, mix_post_g, ffn2_pre_g, ffn2_w_gate, ffn2_w_up, ffn2_w_down, ffn2_post_g):
    b, s, d = x.shape
    t = b * s
    xf = x.reshape(t, d)
    for l in range(ffn1_pre_g.shape[0]):
        xf = _ffn(xf, l, ffn1_pre_g, ffn1_w_gate, ffn1_w_up, ffn1_w_down, ffn1_post_g)
        ya, yb, pc, yd = _mix_in(xf, s, l, mix_pre_g[l], w_in, sc_conv_w[l], sg_ln_w[l], sg_ln_b[l],
                                 sg_w[l], sg_b[l], cm_conv_w[l], cm_conv_b[l], cm_ln_w[l], cm_ln_b[l])
        yc = _mix_c(pc.reshape(b, s, C_COLS), rk_mu[l], rk_w0[l], rk_w_up[l], rk_a0[l], rk_a_up[l],
                    rk_g_up[l], rk_k_k[l], rk_k_a[l], rk_r_k[l].reshape(-1), rk_ln_w[l], rk_ln_b[l])
        xf = _ffn(xf, l, ffn2_pre_g, ffn2_w_gate, ffn2_w_up, ffn2_w_down, ffn2_post_g,
                  mix=((ya, yb, yc.reshape(t, GROUP_W), yd), w_out, mix_post_g))
    return xf.reshape(b, s, d)
```
